```python
import jax, jax.numpy as jnp
from jax import lax
import numpy as np

D_MODEL = 1024
BATCH = 8
SEQ = 8192
DEPTH = 1
DEC_BATCH = 8
DEC_SEQ = 16
PAST_LEN = 1024

CHUNK = 64
POOL_WINDOWS = (2, 4, 8, 16)
N_POOL_GROUPS = len(POOL_WINDOWS)
POOL_WIDTH = D_MODEL // 2
POOL_GROUP = POOL_WIDTH // N_POOL_GROUPS
CONV_WIDTH = D_MODEL - POOL_WIDTH
CONV_KERNEL = 31
POOL_HIST = max(POOL_WINDOWS) - 1
CONV_HIST = CONV_KERNEL - 1
D_FF = 4 * D_MODEL
LN_EPS = 1e-5
DN_ALPHA = (2.0 * DEPTH) ** 0.25
DN_BETA = (8.0 * DEPTH) ** -0.25

kernel_name = "pool_conformer_hybrid_stream_step"


def layer_norm(x, g, b):
    xf = x.astype(jnp.float32)
    mu = jnp.mean(xf, axis=-1, keepdims=True)
    var = jnp.mean(jnp.square(xf - mu), axis=-1, keepdims=True)
    y = (xf - mu) * lax.rsqrt(var + LN_EPS) * g.astype(jnp.float32) + b.astype(jnp.float32)
    return y.astype(x.dtype)


def pool_mixer(u_ext, pos0, w_pool, pool_scale):
    bsz, ext_len, _ = u_ext.shape
    seq = ext_len - POOL_HIST
    uf = u_ext.astype(jnp.float32)
    cs = jnp.pad(jnp.cumsum(uf, axis=1), ((0, 0), (1, 0), (0, 0)))
    pos = pos0 + jnp.arange(seq)
    end = cs[:, POOL_HIST + 1:]
    u_new = uf[:, POOL_HIST:]
    outs = []
    for gi, w in enumerate(POOL_WINDOWS):
        sl = slice(gi * POOL_GROUP, (gi + 1) * POOL_GROUP)
        start = cs[:, POOL_HIST + 1 - w: POOL_HIST + 1 - w + seq, sl]
        count = jnp.minimum(pos + 1, w).astype(jnp.float32)[None, :, None]
        outs.append((end[..., sl] - start) / count - u_new[..., sl])
    d = jnp.stack(outs, axis=2).astype(u_ext.dtype)
    y = jnp.einsum('blgc,gcd->blgd', d, w_pool)
    return y.reshape(bsz, seq, POOL_WIDTH) * pool_scale


def conv_mixer(v_ext, conv_w, conv_b, ln_g, ln_b):
    y = lax.conv_general_dilated(v_ext, conv_w, window_strides=(1,), padding='VALID',
                                 dimension_numbers=('NWC', 'WIO', 'NWC'),
                                 feature_group_count=CONV_WIDTH)
    y = layer_norm(y + conv_b, ln_g, ln_b)
    return jax.nn.silu(y)


def trunk_layer(x, hist_pool, hist_conv, pos0, w_in, b_in, w_pool, pool_scale, conv_w, conv_b,
                conv_ln_g, conv_ln_b, w_out, b_out, ln1_g, ln1_b, w_up, w_down, ln2_g, ln2_b):
    z = x @ w_in + b_in
    u = z[..., :POOL_WIDTH]
    a = z[..., POOL_WIDTH:POOL_WIDTH + CONV_WIDTH]
    g = z[..., POOL_WIDTH + CONV_WIDTH:]
    v = a * jax.nn.sigmoid(g)
    u_ext = jnp.concatenate([hist_pool, u], axis=1)
    v_ext = jnp.concatenate([hist_conv, v], axis=1)
    mix = jnp.concatenate([pool_mixer(u_ext, pos0, w_pool, pool_scale),
                           conv_mixer(v_ext, conv_w, conv_b, conv_ln_g, conv_ln_b)], axis=-1)
    h = layer_norm(DN_ALPHA * x + mix @ w_out + b_out, ln1_g, ln1_b)
    f = jnp.square(jax.nn.relu(h @ w_up)) @ w_down
    y = layer_norm(DN_ALPHA * h + f, ln2_g, ln2_b)
    return y, u_ext[:, -POOL_HIST:], v_ext[:, -CONV_HIST:]


def setup_inputs(seed: int = 0) -> dict:
    key = jax.random.key(seed)
    ks = jax.random.split(key, 20)
    f32 = jnp.float32
    n = lambda k, s, sc: jax.random.normal(k, s, f32) * sc
    d_in = POOL_WIDTH + 2 * CONV_WIDTH
    return {
        "x_prompt": n(ks[0], (BATCH, SEQ, D_MODEL), 1.0),
        "x_sample": n(ks[1], (DEC_BATCH, DEC_SEQ, D_MODEL), 1.0),
        "state_pool": n(ks[2], (DEPTH, DEC_BATCH, POOL_HIST, POOL_WIDTH), 1.0),
        "state_conv": n(ks[3], (DEPTH, DEC_BATCH, CONV_HIST, CONV_WIDTH), 0.5),
        "w_in": n(ks[4], (DEPTH, D_MODEL, d_in), D_MODEL ** -0.5),
        "b_in": n(ks[5], (DEPTH, d_in), 0.02),
        "w_pool": n(ks[6], (DEPTH, N_POOL_GROUPS, POOL_GROUP, POOL_GROUP), POOL_GROUP ** -0.5),
        "pool_scale": 1.0 + n(ks[7], (DEPTH, POOL_WIDTH), 0.1),
        "conv_w": n(ks[8], (DEPTH, CONV_KERNEL, 1, CONV_WIDTH), CONV_KERNEL ** -0.5),
        "conv_b": n(ks[9], (DEPTH, CONV_WIDTH), 0.02),
        "conv_ln_g": 1.0 + n(ks[10], (DEPTH, CONV_WIDTH), 0.05),
        "conv_ln_b": n(ks[11], (DEPTH, CONV_WIDTH), 0.02),
        "w_out": n(ks[12], (DEPTH, D_MODEL, D_MODEL), DN_BETA * D_MODEL ** -0.5),
        "b_out": n(ks[13], (DEPTH, D_MODEL), 0.02),
        "ln1_g": 1.0 + n(ks[14], (DEPTH, D_MODEL), 0.05),
        "ln1_b": n(ks[15], (DEPTH, D_MODEL), 0.02),
        "w_up": n(ks[16], (DEPTH, D_MODEL, D_FF), D_MODEL ** -0.5),
        "w_down": n(ks[17], (DEPTH, D_FF, D_MODEL), DN_BETA * D_FF ** -0.5),
        "ln2_g": 1.0 + n(ks[18], (DEPTH, D_MODEL), 0.05),
        "ln2_b": n(ks[19], (DEPTH, D_MODEL), 0.02),
    }


def reference(x_prompt, x_sample, state_pool, state_conv, w_in, b_in, w_pool, pool_scale,
              conv_w, conv_b, conv_ln_g, conv_ln_b, w_out, b_out, ln1_g, ln1_b,
              w_up, w_down, ln2_g, ln2_b):
    yp, ys = x_prompt, x_sample
    pool_p, conv_p, pool_s, conv_s = [], [], [], []
    for l in range(DEPTH):
        params = (w_in[l], b_in[l], w_pool[l], pool_scale[l], conv_w[l], conv_b[l],
                  conv_ln_g[l], conv_ln_b[l], w_out[l], b_out[l], ln1_g[l], ln1_b[l],
                  w_up[l], w_down[l], ln2_g[l], ln2_b[l])
        zp_pool = jnp.zeros((yp.shape[0], POOL_HIST, POOL_WIDTH), yp.dtype)
        zp_conv = jnp.zeros((yp.shape[0], CONV_HIST, CONV_WIDTH), yp.dtype)
        yp, hp, hc = trunk_layer(yp, zp_pool, zp_conv, 0, *params)
        pool_p.append(hp)
        conv_p.append(hc)
        ys, sp, sc = trunk_layer(ys, state_pool[l].astype(ys.dtype), state_conv[l].astype(ys.dtype),
                                 PAST_LEN, *params)
        pool_s.append(sp)
        conv_s.append(sc)
    return (yp, ys, jnp.stack(pool_p), jnp.stack(conv_p), jnp.stack(pool_s), jnp.stack(conv_s))
```

```python
import functools

import jax
import jax.numpy as jnp
from jax.experimental import pallas as pl
from jax.experimental.pallas import tpu as pltpu

D_MODEL = 1024
POOL_WINDOWS = (2, 4, 8, 16)
N_POOL_GROUPS = len(POOL_WINDOWS)
POOL_WIDTH = D_MODEL // 2
POOL_GROUP = POOL_WIDTH // N_POOL_GROUPS
CONV_WIDTH = D_MODEL - POOL_WIDTH
CONV_KERNEL = 31
POOL_HIST = max(POOL_WINDOWS) - 1
CONV_HIST = CONV_KERNEL - 1
D_FF = 4 * D_MODEL
D_IN = POOL_WIDTH + 2 * CONV_WIDTH
LN_EPS = 1e-5

SUBLANES = 8
POOL_PAD = 16
CONV_PAD = 32
FF_CHUNK = 1024
VMEM_LIMIT_BYTES = 56 * 1024 * 1024


def _layer_norm(x, g, b):
    mu = jnp.mean(x, axis=-1, keepdims=True)
    xc = x - mu
    var = jnp.mean(xc * xc, axis=-1, keepdims=True)
    return xc * jax.lax.rsqrt(var + LN_EPS) * g + b


def _trunk_kernel(x_ref, hp_ref, hc_ref, w_in_ref, b_in_ref, w_pool_ref, pool_scale_ref,
                  conv_w_ref, conv_b_ref, cln_g_ref, cln_b_ref, w_out_ref, b_out_ref,
                  ln1_g_ref, ln1_b_ref, w_up_ref, w_down_ref, ln2_g_ref, ln2_b_ref,
                  y_ref, new_pool_ref, new_conv_ref, uext_ref, vext_ref,
                  *, bt, tile, pos0, alpha):
    j = pl.program_id(1)
    rows = bt * tile
    f32 = jnp.float32
    bf16 = jnp.bfloat16

    @pl.when(j == 0)
    def _():
        uext_ref[:, POOL_PAD - POOL_HIST:POOL_PAD, :] = hp_ref[...]
        vext_ref[:, CONV_PAD - CONV_HIST:CONV_PAD, :] = hc_ref[...]

    x = x_ref[...].reshape(rows, D_MODEL)
    z = jnp.dot(x.astype(bf16), w_in_ref[...], preferred_element_type=f32) + b_in_ref[...]
    u = z[:, :POOL_WIDTH]
    a = z[:, POOL_WIDTH:POOL_WIDTH + CONV_WIDTH]
    g = z[:, POOL_WIDTH + CONV_WIDTH:]
    v = a * jax.nn.sigmoid(g)
    uext_ref[:, POOL_PAD:POOL_PAD + tile, :] = u.reshape(bt, tile, POOL_WIDTH)
    vext_ref[:, CONV_PAD:CONV_PAD + tile, :] = v.reshape(bt, tile, CONV_WIDTH)

    pos = pos0 + j * tile + jax.lax.broadcasted_iota(jnp.int32, (tile, 1), 0)
    pool_parts = []
    for gi, w in enumerate(POOL_WINDOWS):
        lo = gi * POOL_GROUP
        inv_count = 1.0 / jnp.minimum(pos + 1, w).astype(f32)
        s = uext_ref[:, POOL_PAD:POOL_PAD + tile, lo:lo + POOL_GROUP]
        for i in range(1, w):
            s = s + uext_ref[:, POOL_PAD - i:POOL_PAD - i + tile, lo:lo + POOL_GROUP]
        d = s * inv_count[None] - uext_ref[:, POOL_PAD:POOL_PAD + tile, lo:lo + POOL_GROUP]
        pool_parts.append(d.reshape(rows, POOL_GROUP).astype(bf16))
    pool_out = []
    for p in range(N_POOL_GROUPS // 2):
        d2 = jnp.concatenate(pool_parts[2 * p:2 * p + 2], axis=-1)
        pool_out.append(jnp.dot(d2, w_pool_ref[p], preferred_element_type=f32))
    pool_y = jnp.concatenate(pool_out, axis=-1) * pool_scale_ref[...]

    acc = vext_ref[:, CONV_PAD - CONV_HIST:CONV_PAD - CONV_HIST + tile, :] * conv_w_ref[0:1, :][None]
    for k in range(1, CONV_KERNEL):
        s0 = CONV_PAD - CONV_HIST + k
        acc = acc + vext_ref[:, s0:s0 + tile, :] * conv_w_ref[k:k + 1, :][None]
    c = acc.reshape(rows, CONV_WIDTH) + conv_b_ref[...]
    c = jax.nn.silu(_layer_norm(c, cln_g_ref[...], cln_b_ref[...]))

    mix = jnp.concatenate([pool_y.astype(bf16), c.astype(bf16)], axis=-1)
    h = alpha * x + jnp.dot(mix, w_out_ref[...], preferred_element_type=f32) + b_out_ref[...]
    h = _layer_norm(h, ln1_g_ref[...], ln1_b_ref[...])

    hb = h.astype(bf16)
    f = None
    for ci in range(D_FF // FF_CHUNK):
        c0 = ci * FF_CHUNK
        up = jnp.dot(hb, w_up_ref[:, c0:c0 + FF_CHUNK], preferred_element_type=f32)
        act = jnp.square(jnp.maximum(up, 0.0)).astype(bf16)
        part = jnp.dot(act, w_down_ref[c0:c0 + FF_CHUNK, :], preferred_element_type=f32)
        f = part if f is None else f + part
    y = _layer_norm(alpha * h + f, ln2_g_ref[...], ln2_b_ref[...])
    y_ref[...] = y.reshape(bt, tile, D_MODEL)

    last = pl.num_programs(1) - 1

    @pl.when(j == last)
    def _():
        new_pool_ref[...] = uext_ref[:, POOL_PAD + tile - POOL_HIST:POOL_PAD + tile, :]
        new_conv_ref[...] = vext_ref[:, CONV_PAD + tile - CONV_HIST:CONV_PAD + tile, :]

    @pl.when(j != last)
    def _():
        uext_ref[:, POOL_PAD - POOL_HIST:POOL_PAD, :] = (
            uext_ref[:, POOL_PAD + tile - POOL_HIST:POOL_PAD + tile, :])
        vext_ref[:, CONV_PAD - CONV_HIST:CONV_PAD, :] = (
            vext_ref[:, CONV_PAD + tile - CONV_HIST:CONV_PAD + tile, :])


def _const_spec(shape):
    zeros = (0,) * len(shape)
    return pl.BlockSpec(shape, lambda b, j: zeros, pipeline_mode=pl.Buffered(1))


def _trunk_call(x, hist_pool, hist_conv, params, *, bt, tile, pos0, alpha, name):
    batch, seq, _ = x.shape
    assert batch % bt == 0 and seq % tile == 0 and tile % SUBLANES == 0
    assert seq // tile == 1 or tile >= CONV_HIST
    grid = (batch // bt, seq // tile)
    kern = functools.partial(_trunk_kernel, bt=bt, tile=tile, pos0=pos0, alpha=alpha)
    in_specs = [
        pl.BlockSpec((bt, tile, D_MODEL), lambda b, j: (b, j, 0)),
        pl.BlockSpec((bt, POOL_HIST, POOL_WIDTH), lambda b, j: (b, 0, 0)),
        pl.BlockSpec((bt, CONV_HIST, CONV_WIDTH), lambda b, j: (b, 0, 0)),
    ] + [_const_spec(p.shape) for p in params]
    out_specs = [
        pl.BlockSpec((bt, tile, D_MODEL), lambda b, j: (b, j, 0)),
        pl.BlockSpec((bt, POOL_HIST, POOL_WIDTH), lambda b, j: (b, 0, 0)),
        pl.BlockSpec((bt, CONV_HIST, CONV_WIDTH), lambda b, j: (b, 0, 0)),
    ]
    out_shape = [
        jax.ShapeDtypeStruct((batch, seq, D_MODEL), x.dtype),
        jax.ShapeDtypeStruct((batch, POOL_HIST, POOL_WIDTH), x.dtype),
        jax.ShapeDtypeStruct((batch, CONV_HIST, CONV_WIDTH), x.dtype),
    ]
    scratch = [
        pltpu.VMEM((bt, POOL_PAD + tile, POOL_WIDTH), jnp.float32),
        pltpu.VMEM((bt, CONV_PAD + tile, CONV_WIDTH), jnp.float32),
    ]
    return pl.pallas_call(
        kern, grid=grid, in_specs=in_specs, out_specs=out_specs, out_shape=out_shape,
        scratch_shapes=scratch, name=name,
        compiler_params=pltpu.CompilerParams(
            dimension_semantics=("arbitrary", "arbitrary"),
            vmem_limit_bytes=VMEM_LIMIT_BYTES),
    )(x, hist_pool, hist_conv, *params)


def _layer_params(l, w_in, b_in, w_pool, pool_scale, conv_w, conv_b, conv_ln_g, conv_ln_b,
                  w_out, b_out, ln1_g, ln1_b, w_up, w_down, ln2_g, ln2_b):
    bf16 = jnp.bfloat16
    row = lambda p: p[l].reshape(1, -1)
    wp = w_pool[l].astype(bf16)
    zero = jnp.zeros((POOL_GROUP, POOL_GROUP), bf16)
    wp_pairs = jnp.stack([
        jnp.block([[wp[2 * p], zero], [zero, wp[2 * p + 1]]]) for p in range(N_POOL_GROUPS // 2)])
    return (w_in[l].astype(bf16), row(b_in), wp_pairs, row(pool_scale),
            conv_w[l].reshape(CONV_KERNEL, CONV_WIDTH), row(conv_b), row(conv_ln_g), row(conv_ln_b),
            w_out[l].astype(bf16), row(b_out), row(ln1_g), row(ln1_b),
            w_up[l].astype(bf16), w_down[l].astype(bf16), row(ln2_g), row(ln2_b))


def kernel(x_prompt, x_sample, state_pool, state_conv, w_in, b_in, w_pool, pool_scale, conv_w, conv_b, conv_ln_g, conv_ln_b, w_out, b_out, ln1_g, ln1_b, w_up, w_down, ln2_g, ln2_b):
    depth = w_in.shape[0]
    alpha = (2.0 * depth) ** 0.25
    past_len = 1024
    yp, ys = x_prompt, x_sample
    pool_p, conv_p, pool_s, conv_s = [], [], [], []
    for l in range(depth):
        params = _layer_params(l, w_in, b_in, w_pool, pool_scale, conv_w, conv_b, conv_ln_g,
                               conv_ln_b, w_out, b_out, ln1_g, ln1_b, w_up, w_down, ln2_g, ln2_b)
        zp_pool = jnp.zeros((yp.shape[0], POOL_HIST, POOL_WIDTH), yp.dtype)
        zp_conv = jnp.zeros((yp.shape[0], CONV_HIST, CONV_WIDTH), yp.dtype)
        yp, hp, hc = _trunk_call(yp, zp_pool, zp_conv, params, bt=1, tile=512, pos0=0,
                                 alpha=alpha, name="trunk_prompt")
        pool_p.append(hp)
        conv_p.append(hc)
        ys, sp, sc = _trunk_call(ys, state_pool[l].astype(ys.dtype), state_conv[l].astype(ys.dtype),
                                 params, bt=ys.shape[0], tile=ys.shape[1], pos0=past_len,
                                 alpha=alpha, name="trunk_sample")
        pool_s.append(sp)
        conv_s.append(sc)
    return (yp, ys, jnp.stack(pool_p), jnp.stack(conv_p), jnp.stack(pool_s), jnp.stack(conv_s))
```

```python
import functools

import jax
import jax.numpy as jnp
from jax.experimental import pallas as pl
from jax.experimental.pallas import tpu as pltpu

D_MODEL = 1024
POOL_WINDOWS = (2, 4, 8, 16)
N_POOL_GROUPS = len(POOL_WINDOWS)
POOL_WIDTH = D_MODEL // 2
POOL_GROUP = POOL_WIDTH // N_POOL_GROUPS
CONV_WIDTH = D_MODEL - POOL_WIDTH
CONV_KERNEL = 31
POOL_HIST = max(POOL_WINDOWS) - 1
CONV_HIST = CONV_KERNEL - 1
D_FF = 4 * D_MODEL
D_IN = POOL_WIDTH + 2 * CONV_WIDTH
LN_EPS = 1e-5

SUBLANES = 8
LANES = 128
LANE_GROUPS = POOL_WIDTH // LANES
POOL_PAD = 16
CONV_PAD = 32
FF_CHUNK = 1024
VMEM_LIMIT_BYTES = 56 * 1024 * 1024

assert POOL_GROUP == LANES and CONV_WIDTH == POOL_WIDTH


def _layer_norm(x, g, b):
    mu = jnp.mean(x, axis=-1, keepdims=True)
    xc = x - mu
    var = jnp.mean(xc * xc, axis=-1, keepdims=True)
    return xc * jax.lax.rsqrt(var + LN_EPS) * g + b


def _trunk_kernel(x_ref, hp_ref, hc_ref, w_in_ref, b_in_ref, w_pool_ref, pool_scale_ref,
                  conv_w_ref, conv_b_ref, cln_g_ref, cln_b_ref, w_out_ref, b_out_ref,
                  ln1_g_ref, ln1_b_ref, w_up_ref, w_down_ref, ln2_g_ref, ln2_b_ref,
                  y_ref, new_pool_ref, new_conv_ref, uext_ref, vext_ref,
                  *, bt, tile, pos0, alpha):
    j = pl.program_id(1)
    rows = bt * tile
    f32 = jnp.float32
    bf16 = jnp.bfloat16

    @pl.when(j == 0)
    def _():
        for c in range(LANE_GROUPS):
            cs = slice(c * LANES, (c + 1) * LANES)
            uext_ref[c, :, POOL_PAD - POOL_HIST:POOL_PAD, :] = hp_ref[:, :, cs]
            vext_ref[c, :, CONV_PAD - CONV_HIST:CONV_PAD, :] = hc_ref[:, :, cs]

    x = x_ref[...].reshape(rows, D_MODEL)
    z = jnp.dot(x.astype(bf16), w_in_ref[...], preferred_element_type=f32) + b_in_ref[...]
    u = z[:, :POOL_WIDTH]
    a = z[:, POOL_WIDTH:POOL_WIDTH + CONV_WIDTH]
    g = z[:, POOL_WIDTH + CONV_WIDTH:]
    v = a * jax.nn.sigmoid(g)
    for c in range(LANE_GROUPS):
        cs = slice(c * LANES, (c + 1) * LANES)
        uext_ref[c, :, POOL_PAD:POOL_PAD + tile, :] = u[:, cs].reshape(bt, tile, LANES)
        vext_ref[c, :, CONV_PAD:CONV_PAD + tile, :] = v[:, cs].reshape(bt, tile, LANES)

    pos1 = (pos0 + 1 + j * tile + jax.lax.broadcasted_iota(jnp.int32, (tile, LANES), 0)).astype(f32)
    pool_parts = []
    for gi, w in enumerate(POOL_WINDOWS):
        inv_count = 1.0 / jnp.minimum(pos1, float(w))
        s = uext_ref[gi, :, POOL_PAD:POOL_PAD + tile, :]
        for i in range(1, w):
            s = s + uext_ref[gi, :, POOL_PAD - i:POOL_PAD - i + tile, :]
        d = s * inv_count[None] - uext_ref[gi, :, POOL_PAD:POOL_PAD + tile, :]
        pool_parts.append(d.reshape(rows, POOL_GROUP).astype(bf16))
    pool_out = []
    for p in range(N_POOL_GROUPS // 2):
        d2 = jnp.concatenate(pool_parts[2 * p:2 * p + 2], axis=-1)
        pool_out.append(jnp.dot(d2, w_pool_ref[p], preferred_element_type=f32))
    pool_y = jnp.concatenate(pool_out, axis=-1) * pool_scale_ref[...]

    conv_parts = []
    for c in range(LANE_GROUPS):
        cs = slice(c * LANES, (c + 1) * LANES)
        s0 = CONV_PAD - CONV_HIST
        acc = vext_ref[c, :, s0:s0 + tile, :] * conv_w_ref[0:1, cs][None]
        for k in range(1, CONV_KERNEL):
            acc = acc + vext_ref[c, :, s0 + k:s0 + k + tile, :] * conv_w_ref[k:k + 1, cs][None]
        conv_parts.append(acc.reshape(rows, LANES))
    c = jnp.concatenate(conv_parts, axis=-1) + conv_b_ref[...]
    c = jax.nn.silu(_layer_norm(c, cln_g_ref[...], cln_b_ref[...]))

    mix = jnp.concatenate([pool_y.astype(bf16), c.astype(bf16)], axis=-1)
    h = alpha * x + jnp.dot(mix, w_out_ref[...], preferred_element_type=f32) + b_out_ref[...]
    h = _layer_norm(h, ln1_g_ref[...], ln1_b_ref[...])

    hb = h.astype(bf16)
    f = None
    for ci in range(D_FF // FF_CHUNK):
        c0 = ci * FF_CHUNK
        up = jnp.dot(hb, w_up_ref[:, c0:c0 + FF_CHUNK], preferred_element_type=f32)
        act = jnp.square(jnp.maximum(up, 0.0)).astype(bf16)
        part = jnp.dot(act, w_down_ref[c0:c0 + FF_CHUNK, :], preferred_element_type=f32)
        f = part if f is None else f + part
    y = _layer_norm(alpha * h + f, ln2_g_ref[...], ln2_b_ref[...])
    y_ref[...] = y.reshape(bt, tile, D_MODEL)

    for c in range(LANE_GROUPS):
        cs = slice(c * LANES, (c + 1) * LANES)
        u_tail = uext_ref[c, :, POOL_PAD + tile - POOL_HIST:POOL_PAD + tile, :]
        v_tail = vext_ref[c, :, CONV_PAD + tile - CONV_HIST:CONV_PAD + tile, :]
        new_pool_ref[:, :, cs] = u_tail
        new_conv_ref[:, :, cs] = v_tail
        uext_ref[c, :, POOL_PAD - POOL_HIST:POOL_PAD, :] = u_tail
        vext_ref[c, :, CONV_PAD - CONV_HIST:CONV_PAD, :] = v_tail


def _const_spec(shape):
    zeros = (0,) * len(shape)
    return pl.BlockSpec(shape, lambda b, j: zeros, pipeline_mode=pl.Buffered(1))


def _trunk_call(x, hist_pool, hist_conv, params, *, bt, tile, pos0, alpha, name):
    batch, seq, _ = x.shape
    assert batch % bt == 0 and seq % tile == 0 and tile % SUBLANES == 0
    grid = (batch // bt, seq // tile)
    kern = functools.partial(_trunk_kernel, bt=bt, tile=tile, pos0=pos0, alpha=alpha)
    in_specs = [
        pl.BlockSpec((bt, tile, D_MODEL), lambda b, j: (b, j, 0)),
        pl.BlockSpec((bt, POOL_HIST, POOL_WIDTH), lambda b, j: (b, 0, 0)),
        pl.BlockSpec((bt, CONV_HIST, CONV_WIDTH), lambda b, j: (b, 0, 0)),
    ] + [_const_spec(p.shape) for p in params]
    out_specs = [
        pl.BlockSpec((bt, tile, D_MODEL), lambda b, j: (b, j, 0)),
        pl.BlockSpec((bt, POOL_HIST, POOL_WIDTH), lambda b, j: (b, 0, 0)),
        pl.BlockSpec((bt, CONV_HIST, CONV_WIDTH), lambda b, j: (b, 0, 0)),
    ]
    out_shape = [
        jax.ShapeDtypeStruct((batch, seq, D_MODEL), x.dtype),
        jax.ShapeDtypeStruct((batch, POOL_HIST, POOL_WIDTH), x.dtype),
        jax.ShapeDtypeStruct((batch, CONV_HIST, CONV_WIDTH), x.dtype),
    ]
    scratch = [
        pltpu.VMEM((LANE_GROUPS, bt, POOL_PAD + tile, LANES), jnp.float32),
        pltpu.VMEM((LANE_GROUPS, bt, CONV_PAD + tile, LANES), jnp.float32),
    ]
    return pl.pallas_call(
        kern, grid=grid, in_specs=in_specs, out_specs=out_specs, out_shape=out_shape,
        scratch_shapes=scratch, name=name,
        compiler_params=pltpu.CompilerParams(
            dimension_semantics=("arbitrary", "arbitrary"),
            vmem_limit_bytes=VMEM_LIMIT_BYTES),
    )(x, hist_pool, hist_conv, *params)


def _layer_params(l, w_in, b_in, w_pool, pool_scale, conv_w, conv_b, conv_ln_g, conv_ln_b,
                  w_out, b_out, ln1_g, ln1_b, w_up, w_down, ln2_g, ln2_b):
    bf16 = jnp.bfloat16
    row = lambda p: p[l].reshape(1, -1)
    wp = w_pool[l].astype(bf16)
    zero = jnp.zeros((POOL_GROUP, POOL_GROUP), bf16)
    wp_pairs = jnp.stack([
        jnp.block([[wp[2 * p], zero], [zero, wp[2 * p + 1]]]) for p in range(N_POOL_GROUPS // 2)])
    return (w_in[l].astype(bf16), row(b_in), wp_pairs, row(pool_scale),
            conv_w[l].reshape(CONV_KERNEL, CONV_WIDTH), row(conv_b), row(conv_ln_g), row(conv_ln_b),
            w_out[l].astype(bf16), row(b_out), row(ln1_g), row(ln1_b),
            w_up[l].astype(bf16), w_down[l].astype(bf16), row(ln2_g), row(ln2_b))


def kernel(x_prompt, x_sample, state_pool, state_conv, w_in, b_in, w_pool, pool_scale, conv_w, conv_b, conv_ln_g, conv_ln_b, w_out, b_out, ln1_g, ln1_b, w_up, w_down, ln2_g, ln2_b):
    depth = w_in.shape[0]
    alpha = (2.0 * depth) ** 0.25
    past_len = 1024
    yp, ys = x_prompt, x_sample
    pool_p, conv_p, pool_s, conv_s = [], [], [], []
    for l in range(depth):
        params = _layer_params(l, w_in, b_in, w_pool, pool_scale, conv_w, conv_b, conv_ln_g,
                               conv_ln_b, w_out, b_out, ln1_g, ln1_b, w_up, w_down, ln2_g, ln2_b)
        zp_pool = jnp.zeros((yp.shape[0], POOL_HIST, POOL_WIDTH), yp.dtype)
        zp_conv = jnp.zeros((yp.shape[0], CONV_HIST, CONV_WIDTH), yp.dtype)
        yp, hp, hc = _trunk_call(yp, zp_pool, zp_conv, params, bt=1, tile=512, pos0=0,
                                 alpha=alpha, name="trunk_prompt")
        pool_p.append(hp)
        conv_p.append(hc)
        ys, sp, sc = _trunk_call(ys, state_pool[l].astype(ys.dtype), state_conv[l].astype(ys.dtype),
                                 params, bt=ys.shape[0], tile=ys.shape[1], pos0=past_len,
                                 alpha=alpha, name="trunk_sample")
        pool_s.append(sp)
        conv_s.append(sc)
    return (yp, ys, jnp.stack(pool_p), jnp.stack(conv_p), jnp.stack(pool_s), jnp.stack(conv_s))
```

```python
import functools

import jax
import jax.numpy as jnp
from jax.experimental import pallas as pl
from jax.experimental.pallas import tpu as pltpu

D_MODEL = 1024
POOL_WINDOWS = (2, 4, 8, 16)
N_POOL_GROUPS = len(POOL_WINDOWS)
POOL_WIDTH = D_MODEL // 2
POOL_GROUP = POOL_WIDTH // N_POOL_GROUPS
CONV_WIDTH = D_MODEL - POOL_WIDTH
CONV_KERNEL = 31
POOL_HIST = max(POOL_WINDOWS) - 1
CONV_HIST = CONV_KERNEL - 1
D_FF = 4 * D_MODEL
D_IN = POOL_WIDTH + 2 * CONV_WIDTH
LN_EPS = 1e-5

SUBLANES = 8
LANES = 128
LANE_GROUPS = POOL_WIDTH // LANES
POOL_PAD = 16
CONV_PAD = 32
FF_CHUNK = 1024
VMEM_LIMIT_BYTES = 56 * 1024 * 1024

assert POOL_GROUP == LANES and CONV_WIDTH == POOL_WIDTH


def _layer_norm(x, g, b):
    mu = jnp.mean(x, axis=-1, keepdims=True)
    xc = x - mu
    var = jnp.mean(xc * xc, axis=-1, keepdims=True)
    return xc * jax.lax.rsqrt(var + LN_EPS) * g + b


def _zero_bits_after(x):
    bits = jax.lax.bitcast_convert_type(x, jnp.uint32)
    return jax.lax.shift_right_logical(jax.lax.shift_right_logical(bits, jnp.uint32(16)), jnp.uint32(16))


def _trunk_kernel(x_ref, hp_ref, hc_ref, w_in_ref, b_in_ref, w_pool_ref, pool_scale_ref,
                  conv_w_ref, conv_b_ref, cln_g_ref, cln_b_ref, w_out_ref, b_out_ref,
                  ln1_g_ref, ln1_b_ref, w_up_ref, w_down_ref, ln2_g_ref, ln2_b_ref,
                  y_ref, new_pool_ref, new_conv_ref, uext_ref, vext_ref, h_ref, hb_ref, r_ref, taps_ref,
                  *, bt, tile, n_tiles, pos0, alpha):
    s = pl.program_id(0)
    last_tile = pl.num_programs(0) - 2
    j = jax.lax.rem(jnp.minimum(s, last_tile), jnp.int32(n_tiles))
    rows = bt * tile
    f32 = jnp.float32
    bf16 = jnp.bfloat16

    @pl.when(s == 0)
    def _():
        h_ref[...] = jnp.zeros_like(h_ref)

    @pl.when(j == 0)
    def _():
        for c in range(LANE_GROUPS):
            cs = slice(c * LANES, (c + 1) * LANES)
            uext_ref[c, :, POOL_PAD - POOL_HIST:POOL_PAD, :] = hp_ref[:, :, cs]
            vext_ref[c, :, CONV_PAD - CONV_HIST:CONV_PAD, :] = hc_ref[:, :, cs]

    h_prev = _layer_norm(h_ref[...], ln1_g_ref[...], ln1_b_ref[...])
    hb_ref[...] = h_prev.astype(bf16)
    r_ref[...] = alpha * h_prev

    x = x_ref[...].reshape(rows, D_MODEL)
    z = jnp.dot(x.astype(bf16), w_in_ref[...], preferred_element_type=f32) + b_in_ref[...]
    u = z[:, :POOL_WIDTH]
    a = z[:, POOL_WIDTH:POOL_WIDTH + CONV_WIDTH]
    g = z[:, POOL_WIDTH + CONV_WIDTH:]
    v = a * jax.nn.sigmoid(g)
    for c in range(LANE_GROUPS):
        cs = slice(c * LANES, (c + 1) * LANES)
        uext_ref[c, :, POOL_PAD:POOL_PAD + tile, :] = u[:, cs].reshape(bt, tile, LANES)
        vext_ref[c, :, CONV_PAD:CONV_PAD + tile, :] = v[:, cs].reshape(bt, tile, LANES)

    def ffn_up(ci):
        up = jnp.dot(hb_ref[...], w_up_ref[:, ci * FF_CHUNK:(ci + 1) * FF_CHUNK], preferred_element_type=f32)
        ffn_events.append(_zero_bits_after(up[0:SUBLANES, 0:LANES]))
        return jnp.square(jnp.maximum(up, 0.0)).astype(bf16)

    def ffn_down(ci, act, f):
        part = jnp.dot(act, w_down_ref[ci * FF_CHUNK:(ci + 1) * FF_CHUNK, :], preferred_element_type=f32)
        ffn_events.append(_zero_bits_after(part[0:SUBLANES, 0:LANES]))
        return part if f is None else f + part

    n_chunks = D_FF // FF_CHUNK
    ffn_events = []
    f = None
    act = ffn_up(0)
    for ci in range(1, n_chunks):
        act_next = ffn_up(ci)
        f = ffn_down(ci - 1, act, f)
        act = act_next
    f = ffn_down(n_chunks - 1, act, f)
    y = _layer_norm(r_ref[...] + f, ln2_g_ref[...], ln2_b_ref[...])
    y_ref[...] = y.reshape(bt, tile, D_MODEL)

    pos1 = (pos0 + 1 + j * tile + jax.lax.broadcasted_iota(jnp.int32, (tile, LANES), 0)).astype(f32)
    pool_parts = []
    for gi, w in enumerate(POOL_WINDOWS):
        inv_count = 1.0 / jnp.minimum(pos1, float(w))
        wsum = uext_ref[gi, :, POOL_PAD:POOL_PAD + tile, :]
        for i in range(1, w):
            wsum = wsum + uext_ref[gi, :, POOL_PAD - i:POOL_PAD - i + tile, :]
        d = wsum * inv_count[None] - uext_ref[gi, :, POOL_PAD:POOL_PAD + tile, :]
        pool_parts.append(d.reshape(rows, POOL_GROUP).astype(bf16))
    pool_out = []
    for p in range(N_POOL_GROUPS // 2):
        d2 = jnp.concatenate(pool_parts[2 * p:2 * p + 2], axis=-1)
        pool_out.append(jnp.dot(d2, w_pool_ref[p], preferred_element_type=f32))
    pool_y = jnp.concatenate(pool_out, axis=-1) * pool_scale_ref[...]

    conv_parts = []
    for c in range(LANE_GROUPS):
        cs = slice(c * LANES, (c + 1) * LANES)
        s0 = CONV_PAD - CONV_HIST
        zero_bits = ffn_events[c][0:1, :]
        taps_bits = jax.lax.bitcast_convert_type(conv_w_ref[:, cs], jnp.uint32) + zero_bits
        taps_ref[0:CONV_KERNEL, cs] = jax.lax.bitcast_convert_type(taps_bits, f32)
        acc = vext_ref[c, :, s0:s0 + tile, :] * taps_ref[0:1, cs][None]
        for k in range(1, CONV_KERNEL):
            acc = acc + vext_ref[c, :, s0 + k:s0 + k + tile, :] * taps_ref[k:k + 1, cs][None]
        conv_parts.append(acc.reshape(rows, LANES))
    c = jnp.concatenate(conv_parts, axis=-1) + conv_b_ref[...]
    c = jax.nn.silu(_layer_norm(c, cln_g_ref[...], cln_b_ref[...]))

    mix = jnp.concatenate([pool_y.astype(bf16), c.astype(bf16)], axis=-1)
    h_ref[...] = alpha * x + jnp.dot(mix, w_out_ref[...], preferred_element_type=f32) + b_out_ref[...]

    for c in range(LANE_GROUPS):
        cs = slice(c * LANES, (c + 1) * LANES)
        u_tail = uext_ref[c, :, POOL_PAD + tile - POOL_HIST:POOL_PAD + tile, :]
        v_tail = vext_ref[c, :, CONV_PAD + tile - CONV_HIST:CONV_PAD + tile, :]
        new_pool_ref[:, :, cs] = u_tail
        new_conv_ref[:, :, cs] = v_tail
        uext_ref[c, :, POOL_PAD - POOL_HIST:POOL_PAD, :] = u_tail
        vext_ref[c, :, CONV_PAD - CONV_HIST:CONV_PAD, :] = v_tail


def _const_spec(shape):
    zeros = (0,) * len(shape)
    return pl.BlockSpec(shape, lambda s: zeros, pipeline_mode=pl.Buffered(1))


def _trunk_call(x, hist_pool, hist_conv, params, *, bt, tile, pos0, alpha, name):
    batch, seq, _ = x.shape
    assert batch % bt == 0 and seq % tile == 0 and tile % SUBLANES == 0
    n_tiles = seq // tile
    n_total = (batch // bt) * n_tiles
    grid = (n_total + 1,)
    mix_tile = lambda s: jnp.minimum(s, n_total - 1)
    ffn_tile = lambda s: jnp.maximum(s - 1, 0)
    kern = functools.partial(_trunk_kernel, bt=bt, tile=tile, n_tiles=n_tiles, pos0=pos0, alpha=alpha)
    in_specs = [
        pl.BlockSpec((bt, tile, D_MODEL), lambda s: (mix_tile(s) // n_tiles, mix_tile(s) % n_tiles, 0)),
        pl.BlockSpec((bt, POOL_HIST, POOL_WIDTH), lambda s: (mix_tile(s) // n_tiles, 0, 0)),
        pl.BlockSpec((bt, CONV_HIST, CONV_WIDTH), lambda s: (mix_tile(s) // n_tiles, 0, 0)),
    ] + [_const_spec(p.shape) for p in params]
    out_specs = [
        pl.BlockSpec((bt, tile, D_MODEL), lambda s: (ffn_tile(s) // n_tiles, ffn_tile(s) % n_tiles, 0)),
        pl.BlockSpec((bt, POOL_HIST, POOL_WIDTH), lambda s: (mix_tile(s) // n_tiles, 0, 0)),
        pl.BlockSpec((bt, CONV_HIST, CONV_WIDTH), lambda s: (mix_tile(s) // n_tiles, 0, 0)),
    ]
    out_shape = [
        jax.ShapeDtypeStruct((batch, seq, D_MODEL), x.dtype),
        jax.ShapeDtypeStruct((batch, POOL_HIST, POOL_WIDTH), x.dtype),
        jax.ShapeDtypeStruct((batch, CONV_HIST, CONV_WIDTH), x.dtype),
    ]
    scratch = [
        pltpu.VMEM((LANE_GROUPS, bt, POOL_PAD + tile, LANES), jnp.float32),
        pltpu.VMEM((LANE_GROUPS, bt, CONV_PAD + tile, LANES), jnp.float32),
        pltpu.VMEM((bt * tile, D_MODEL), jnp.float32),
        pltpu.VMEM((bt * tile, D_MODEL), jnp.bfloat16),
        pltpu.VMEM((bt * tile, D_MODEL), jnp.float32),
        pltpu.VMEM((CONV_KERNEL + 1, CONV_WIDTH), jnp.float32),
    ]
    return pl.pallas_call(
        kern, grid=grid, in_specs=in_specs, out_specs=out_specs, out_shape=out_shape,
        scratch_shapes=scratch, name=name,
        compiler_params=pltpu.CompilerParams(
            dimension_semantics=("arbitrary",),
            vmem_limit_bytes=VMEM_LIMIT_BYTES),
    )(x, hist_pool, hist_conv, *params)


def _layer_params(l, w_in, b_in, w_pool, pool_scale, conv_w, conv_b, conv_ln_g, conv_ln_b,
                  w_out, b_out, ln1_g, ln1_b, w_up, w_down, ln2_g, ln2_b):
    bf16 = jnp.bfloat16
    row = lambda p: p[l].reshape(1, -1)
    wp = w_pool[l].astype(bf16)
    zero = jnp.zeros((POOL_GROUP, POOL_GROUP), bf16)
    wp_pairs = jnp.stack([
        jnp.block([[wp[2 * p], zero], [zero, wp[2 * p + 1]]]) for p in range(N_POOL_GROUPS // 2)])
    return (w_in[l].astype(bf16), row(b_in), wp_pairs, row(pool_scale),
            conv_w[l].reshape(CONV_KERNEL, CONV_WIDTH), row(conv_b), row(conv_ln_g), row(conv_ln_b),
            w_out[l].astype(bf16), row(b_out), row(ln1_g), row(ln1_b),
            w_up[l].astype(bf16), w_down[l].astype(bf16), row(ln2_g), row(ln2_b))


def kernel(x_prompt, x_sample, state_pool, state_conv, w_in, b_in, w_pool, pool_scale, conv_w, conv_b, conv_ln_g, conv_ln_b, w_out, b_out, ln1_g, ln1_b, w_up, w_down, ln2_g, ln2_b):
    depth = w_in.shape[0]
    alpha = (2.0 * depth) ** 0.25
    past_len = 1024
    yp, ys = x_prompt, x_sample
    pool_p, conv_p, pool_s, conv_s = [], [], [], []
    for l in range(depth):
        params = _layer_params(l, w_in, b_in, w_pool, pool_scale, conv_w, conv_b, conv_ln_g,
                               conv_ln_b, w_out, b_out, ln1_g, ln1_b, w_up, w_down, ln2_g, ln2_b)
        zp_pool = jnp.zeros((yp.shape[0], POOL_HIST, POOL_WIDTH), yp.dtype)
        zp_conv = jnp.zeros((yp.shape[0], CONV_HIST, CONV_WIDTH), yp.dtype)
        yp, hp, hc = _trunk_call(yp, zp_pool, zp_conv, params, bt=1, tile=512, pos0=0,
                                 alpha=alpha, name="trunk_prompt")
        pool_p.append(hp)
        conv_p.append(hc)
        ys, sp, sc = _trunk_call(ys, state_pool[l].astype(ys.dtype), state_conv[l].astype(ys.dtype),
                                 params, bt=ys.shape[0], tile=ys.shape[1], pos0=past_len,
                                 alpha=alpha, name="trunk_sample")
        pool_s.append(sp)
        conv_s.append(sc)
    return (yp, ys, jnp.stack(pool_p), jnp.stack(conv_p), jnp.stack(pool_s), jnp.stack(conv_s))
```

```python
import functools

import jax
import jax.numpy as jnp
from jax.experimental import pallas as pl
from jax.experimental.pallas import tpu as pltpu

D_MODEL = 1024
POOL_WINDOWS = (2, 4, 8, 16)
N_POOL_GROUPS = len(POOL_WINDOWS)
POOL_WIDTH = D_MODEL // 2
POOL_GROUP = POOL_WIDTH // N_POOL_GROUPS
CONV_WIDTH = D_MODEL - POOL_WIDTH
CONV_KERNEL = 31
POOL_HIST = max(POOL_WINDOWS) - 1
CONV_HIST = CONV_KERNEL - 1
D_FF = 4 * D_MODEL
D_IN = POOL_WIDTH + 2 * CONV_WIDTH
LN_EPS = 1e-5

SUBLANES = 8
LANES = 128
LANE_GROUPS = POOL_WIDTH // LANES
POOL_PAD = 16
CONV_PAD = 32
FF_CHUNK = 1024
DOT_COLS = 512
VMEM_LIMIT_BYTES = 56 * 1024 * 1024

assert POOL_GROUP == LANES and CONV_WIDTH == POOL_WIDTH


def _layer_norm(x, g, b):
    mu = jnp.mean(x, axis=-1, keepdims=True)
    xc = x - mu
    var = jnp.mean(xc * xc, axis=-1, keepdims=True)
    return xc * jax.lax.rsqrt(var + LN_EPS) * g + b


def _zero_bits_after(x):
    bits = jax.lax.bitcast_convert_type(x, jnp.uint32)
    return jax.lax.shift_right_logical(jax.lax.shift_right_logical(bits, jnp.uint32(16)), jnp.uint32(16))


def _dot_cols(lhs, w_blocks):
    return jnp.concatenate([jnp.dot(lhs, w, preferred_element_type=jnp.float32) for w in w_blocks], axis=-1)


def _trunk_kernel(x_ref, hp_ref, hc_ref, w_in_ref, b_in_ref, w_pool_ref, pool_scale_ref,
                  conv_w_ref, conv_b_ref, cln_g_ref, cln_b_ref, w_out_ref, b_out_ref,
                  ln1_g_ref, ln1_b_ref, w_up_ref, w_down_ref, ln2_g_ref, ln2_b_ref,
                  y_ref, new_pool_ref, new_conv_ref,
                  uext_ref, vext_ref, h_ref, hb_ref, r_ref, taps_ref, act_ref,
                  *, bt, tile, n_tiles, pos0, alpha):
    s = pl.program_id(0)
    last_tile = pl.num_programs(0) - 2
    j = jax.lax.rem(jnp.minimum(s, last_tile), jnp.int32(n_tiles))
    rows = bt * tile
    f32 = jnp.float32
    bf16 = jnp.bfloat16

    @pl.when(s == 0)
    def _():
        h_ref[...] = jnp.zeros_like(h_ref)

    @pl.when(j == 0)
    def _():
        for c in range(LANE_GROUPS):
            cs = slice(c * LANES, (c + 1) * LANES)
            uext_ref[c, :, POOL_PAD - POOL_HIST:POOL_PAD, :] = hp_ref[:, :, cs]
            vext_ref[c, :, CONV_PAD - CONV_HIST:CONV_PAD, :] = hc_ref[:, :, cs]

    h_prev = _layer_norm(h_ref[...], ln1_g_ref[...], ln1_b_ref[...])
    hb_ref[...] = h_prev.astype(bf16)
    r_ref[...] = alpha * h_prev

    x = x_ref[...].reshape(rows, D_MODEL)
    z = jnp.dot(x.astype(bf16), w_in_ref[...], preferred_element_type=f32) + b_in_ref[...]
    u = z[:, :POOL_WIDTH]
    a = z[:, POOL_WIDTH:POOL_WIDTH + CONV_WIDTH]
    g = z[:, POOL_WIDTH + CONV_WIDTH:]
    v = a * jax.nn.sigmoid(g)
    for c in range(LANE_GROUPS):
        cs = slice(c * LANES, (c + 1) * LANES)
        uext_ref[c, :, POOL_PAD:POOL_PAD + tile, :] = u[:, cs].reshape(bt, tile, LANES)
        vext_ref[c, :, CONV_PAD:CONV_PAD + tile, :] = v[:, cs].reshape(bt, tile, LANES)

    def ffn_up(ci):
        blocks = FF_CHUNK // DOT_COLS
        up = _dot_cols(hb_ref[...], [w_up_ref[ci * blocks + n] for n in range(blocks)])
        ffn_events.append(_zero_bits_after(up[0:SUBLANES, 0:LANES]))
        return jnp.square(jnp.maximum(up, 0.0)).astype(bf16)

    ffn_events = []
    for ci in range(D_FF // FF_CHUNK):
        act_ref[:, ci * FF_CHUNK:(ci + 1) * FF_CHUNK] = ffn_up(ci)
    f = _dot_cols(act_ref[...], [w_down_ref[n] for n in range(D_MODEL // DOT_COLS)])
    y = _layer_norm(r_ref[...] + f, ln2_g_ref[...], ln2_b_ref[...])
    y_ref[...] = y.reshape(bt, tile, D_MODEL)

    pos1 = (pos0 + 1 + j * tile + jax.lax.broadcasted_iota(jnp.int32, (tile, LANES), 0)).astype(f32)
    pool_parts = []
    for gi, w in enumerate(POOL_WINDOWS):
        inv_count = 1.0 / jnp.minimum(pos1, float(w))
        wsum = uext_ref[gi, :, POOL_PAD:POOL_PAD + tile, :]
        for i in range(1, w):
            wsum = wsum + uext_ref[gi, :, POOL_PAD - i:POOL_PAD - i + tile, :]
        d = wsum * inv_count[None] - uext_ref[gi, :, POOL_PAD:POOL_PAD + tile, :]
        pool_parts.append(d.reshape(rows, POOL_GROUP).astype(bf16))
    pool_out = []
    for p in range(N_POOL_GROUPS // 2):
        d2 = jnp.concatenate(pool_parts[2 * p:2 * p + 2], axis=-1)
        pool_out.append(jnp.dot(d2, w_pool_ref[p], preferred_element_type=f32))
    pool_y = jnp.concatenate(pool_out, axis=-1) * pool_scale_ref[...]

    conv_parts = []
    for c in range(LANE_GROUPS):
        cs = slice(c * LANES, (c + 1) * LANES)
        s0 = CONV_PAD - CONV_HIST
        zero_bits = ffn_events[c][0:1, :]
        taps_bits = jax.lax.bitcast_convert_type(conv_w_ref[:, cs], jnp.uint32) + zero_bits
        taps_ref[0:CONV_KERNEL, cs] = jax.lax.bitcast_convert_type(taps_bits, f32)
        acc = vext_ref[c, :, s0:s0 + tile, :] * taps_ref[0:1, cs][None]
        for k in range(1, CONV_KERNEL):
            acc = acc + vext_ref[c, :, s0 + k:s0 + k + tile, :] * taps_ref[k:k + 1, cs][None]
        conv_parts.append(acc.reshape(rows, LANES))
    c = jnp.concatenate(conv_parts, axis=-1) + conv_b_ref[...]
    c = jax.nn.silu(_layer_norm(c, cln_g_ref[...], cln_b_ref[...]))

    mix = jnp.concatenate([pool_y.astype(bf16), c.astype(bf16)], axis=-1)
    proj = _dot_cols(mix, [w_out_ref[n] for n in range(D_MODEL // DOT_COLS)])
    h_ref[...] = alpha * x + proj + b_out_ref[...]

    for c in range(LANE_GROUPS):
        cs = slice(c * LANES, (c + 1) * LANES)
        u_tail = uext_ref[c, :, POOL_PAD + tile - POOL_HIST:POOL_PAD + tile, :]
        v_tail = vext_ref[c, :, CONV_PAD + tile - CONV_HIST:CONV_PAD + tile, :]
        new_pool_ref[:, :, cs] = u_tail
        new_conv_ref[:, :, cs] = v_tail
        uext_ref[c, :, POOL_PAD - POOL_HIST:POOL_PAD, :] = u_tail
        vext_ref[c, :, CONV_PAD - CONV_HIST:CONV_PAD, :] = v_tail


def _const_spec(shape):
    zeros = (0,) * len(shape)
    return pl.BlockSpec(shape, lambda s: zeros, pipeline_mode=pl.Buffered(1))


def _trunk_call(x, hist_pool, hist_conv, params, *, bt, tile, pos0, alpha, name):
    batch, seq, _ = x.shape
    assert batch % bt == 0 and seq % tile == 0 and tile % SUBLANES == 0
    n_tiles = seq // tile
    n_total = (batch // bt) * n_tiles
    grid = (n_total + 1,)
    mix_tile = lambda s: jnp.minimum(s, n_total - 1)
    ffn_tile = lambda s: jnp.maximum(s - 1, 0)
    kern = functools.partial(_trunk_kernel, bt=bt, tile=tile, n_tiles=n_tiles, pos0=pos0, alpha=alpha)
    in_specs = [
        pl.BlockSpec((bt, tile, D_MODEL), lambda s: (mix_tile(s) // n_tiles, mix_tile(s) % n_tiles, 0)),
        pl.BlockSpec((bt, POOL_HIST, POOL_WIDTH), lambda s: (mix_tile(s) // n_tiles, 0, 0)),
        pl.BlockSpec((bt, CONV_HIST, CONV_WIDTH), lambda s: (mix_tile(s) // n_tiles, 0, 0)),
    ] + [_const_spec(p.shape) for p in params]
    out_specs = [
        pl.BlockSpec((bt, tile, D_MODEL), lambda s: (ffn_tile(s) // n_tiles, ffn_tile(s) % n_tiles, 0)),
        pl.BlockSpec((bt, POOL_HIST, POOL_WIDTH), lambda s: (mix_tile(s) // n_tiles, 0, 0)),
        pl.BlockSpec((bt, CONV_HIST, CONV_WIDTH), lambda s: (mix_tile(s) // n_tiles, 0, 0)),
    ]
    out_shape = [
        jax.ShapeDtypeStruct((batch, seq, D_MODEL), x.dtype),
        jax.ShapeDtypeStruct((batch, POOL_HIST, POOL_WIDTH), x.dtype),
        jax.ShapeDtypeStruct((batch, CONV_HIST, CONV_WIDTH), x.dtype),
    ]
    scratch = [
        pltpu.VMEM((LANE_GROUPS, bt, POOL_PAD + tile, LANES), jnp.float32),
        pltpu.VMEM((LANE_GROUPS, bt, CONV_PAD + tile, LANES), jnp.float32),
        pltpu.VMEM((bt * tile, D_MODEL), jnp.float32),
        pltpu.VMEM((bt * tile, D_MODEL), jnp.bfloat16),
        pltpu.VMEM((bt * tile, D_MODEL), jnp.float32),
        pltpu.VMEM((CONV_KERNEL + 1, CONV_WIDTH), jnp.float32),
        pltpu.VMEM((bt * tile, D_FF), jnp.bfloat16),
    ]
    return pl.pallas_call(
        kern, grid=grid, in_specs=in_specs, out_specs=out_specs, out_shape=out_shape,
        scratch_shapes=scratch, name=name,
        compiler_params=pltpu.CompilerParams(
            dimension_semantics=("arbitrary",),
            vmem_limit_bytes=VMEM_LIMIT_BYTES),
    )(x, hist_pool, hist_conv, *params)


def _layer_params(l, w_in, b_in, w_pool, pool_scale, conv_w, conv_b, conv_ln_g, conv_ln_b,
                  w_out, b_out, ln1_g, ln1_b, w_up, w_down, ln2_g, ln2_b):
    bf16 = jnp.bfloat16
    row = lambda p: p[l].reshape(1, -1)
    wp = w_pool[l].astype(bf16)
    zero = jnp.zeros((POOL_GROUP, POOL_GROUP), bf16)
    wp_pairs = jnp.stack([
        jnp.block([[wp[2 * p], zero], [zero, wp[2 * p + 1]]]) for p in range(N_POOL_GROUPS // 2)])
    col_blocks = lambda w: w.astype(bf16).reshape(w.shape[0], -1, DOT_COLS).transpose(1, 0, 2)
    return (w_in[l].astype(bf16), row(b_in), wp_pairs, row(pool_scale),
            conv_w[l].reshape(CONV_KERNEL, CONV_WIDTH), row(conv_b), row(conv_ln_g), row(conv_ln_b),
            col_blocks(w_out[l]), row(b_out), row(ln1_g), row(ln1_b),
            col_blocks(w_up[l]), col_blocks(w_down[l]), row(ln2_g), row(ln2_b))


def kernel(x_prompt, x_sample, state_pool, state_conv, w_in, b_in, w_pool, pool_scale, conv_w, conv_b, conv_ln_g, conv_ln_b, w_out, b_out, ln1_g, ln1_b, w_up, w_down, ln2_g, ln2_b):
    depth = w_in.shape[0]
    alpha = (2.0 * depth) ** 0.25
    past_len = 1024
    yp, ys = x_prompt, x_sample
    pool_p, conv_p, pool_s, conv_s = [], [], [], []
    for l in range(depth):
        params = _layer_params(l, w_in, b_in, w_pool, pool_scale, conv_w, conv_b, conv_ln_g,
                               conv_ln_b, w_out, b_out, ln1_g, ln1_b, w_up, w_down, ln2_g, ln2_b)
        zp_pool = jnp.zeros((yp.shape[0], POOL_HIST, POOL_WIDTH), yp.dtype)
        zp_conv = jnp.zeros((yp.shape[0], CONV_HIST, CONV_WIDTH), yp.dtype)
        yp, hp, hc = _trunk_call(yp, zp_pool, zp_conv, params, bt=1, tile=512, pos0=0,
                                 alpha=alpha, name="trunk_prompt")
        pool_p.append(hp)
        conv_p.append(hc)
        ys, sp, sc = _trunk_call(ys, state_pool[l].astype(ys.dtype), state_conv[l].astype(ys.dtype),
                                 params, bt=ys.shape[0], tile=ys.shape[1], pos0=past_len,
                                 alpha=alpha, name="trunk_sample")
        pool_s.append(sp)
        conv_s.append(sc)
    stack = lambda parts: parts[0][None] if depth == 1 else jnp.stack(parts)
    return (yp, ys, stack(pool_p), stack(conv_p), stack(pool_s), stack(conv_s))
```

```python
import functools

import jax
import jax.numpy as jnp
from jax.experimental import pallas as pl
from jax.experimental.pallas import tpu as pltpu

D_MODEL = 1024
POOL_WINDOWS = (2, 4, 8, 16)
N_POOL_GROUPS = len(POOL_WINDOWS)
POOL_WIDTH = D_MODEL // 2
POOL_GROUP = POOL_WIDTH // N_POOL_GROUPS
CONV_WIDTH = D_MODEL - POOL_WIDTH
CONV_KERNEL = 31
POOL_HIST = max(POOL_WINDOWS) - 1
CONV_HIST = CONV_KERNEL - 1
D_FF = 4 * D_MODEL
D_IN = POOL_WIDTH + 2 * CONV_WIDTH
LN_EPS = 1e-5

SUBLANES = 8
LANES = 128
LANE_GROUPS = POOL_WIDTH // LANES
POOL_PAD = 16
CONV_PAD = 32
FF_CHUNK = 1024
DOT_COLS = 512
VMEM_LIMIT_BYTES = 56 * 1024 * 1024

assert POOL_GROUP == LANES and CONV_WIDTH == POOL_WIDTH


def _layer_norm(x, g, b):
    mu = jnp.mean(x, axis=-1, keepdims=True)
    xc = x - mu
    var = jnp.mean(xc * xc, axis=-1, keepdims=True)
    return xc * jax.lax.rsqrt(var + LN_EPS) * g + b


def _zero_bits_after(x):
    bits = jax.lax.bitcast_convert_type(x, jnp.uint32)
    return jax.lax.shift_right_logical(jax.lax.shift_right_logical(bits, jnp.uint32(16)), jnp.uint32(16))


def _dot_cols(lhs, w_blocks):
    return jnp.concatenate([jnp.dot(lhs, w, preferred_element_type=jnp.float32) for w in w_blocks], axis=-1)


N_OUT_BLOCKS = D_MODEL // DOT_COLS
N_UP_BLOCKS = D_FF // DOT_COLS


def _trunk_kernel(*refs, bt, tile, n_tiles, pos0, alpha):
    refs = list(refs)
    take = lambda n: [refs.pop(0) for _ in range(n)]
    (x_ref, hp_ref, hc_ref, w_in_ref, b_in_ref, w_pool_ref, pool_scale_ref,
     conv_w_ref, conv_b_ref, cln_g_ref, cln_b_ref) = take(11)
    w_out_ref = take(N_OUT_BLOCKS)
    b_out_ref, ln1_g_ref, ln1_b_ref = take(3)
    w_up_ref = take(N_UP_BLOCKS)
    w_down_ref = take(N_OUT_BLOCKS)
    ln2_g_ref, ln2_b_ref = take(2)
    y_ref, new_pool_ref, new_conv_ref = take(3)
    uext_ref, vext_ref, h_ref, hb_ref, r_ref, taps_ref, act_ref = refs
    s = pl.program_id(0)
    last_tile = pl.num_programs(0) - 2
    j = jax.lax.rem(jnp.minimum(s, last_tile), jnp.int32(n_tiles))
    rows = bt * tile
    f32 = jnp.float32
    bf16 = jnp.bfloat16

    @pl.when(s == 0)
    def _():
        h_ref[...] = jnp.zeros_like(h_ref)

    @pl.when(j == 0)
    def _():
        for c in range(LANE_GROUPS):
            cs = slice(c * LANES, (c + 1) * LANES)
            uext_ref[c, :, POOL_PAD - POOL_HIST:POOL_PAD, :] = hp_ref[:, :, cs]
            vext_ref[c, :, CONV_PAD - CONV_HIST:CONV_PAD, :] = hc_ref[:, :, cs]

    h_prev = _layer_norm(h_ref[...], ln1_g_ref[...], ln1_b_ref[...])
    hb_ref[...] = h_prev.astype(bf16)
    r_ref[...] = alpha * h_prev

    x = x_ref[...].reshape(rows, D_MODEL)
    z = jnp.dot(x.astype(bf16), w_in_ref[...], preferred_element_type=f32) + b_in_ref[...]
    u = z[:, :POOL_WIDTH]
    a = z[:, POOL_WIDTH:POOL_WIDTH + CONV_WIDTH]
    g = z[:, POOL_WIDTH + CONV_WIDTH:]
    v = a * jax.nn.sigmoid(g)
    for c in range(LANE_GROUPS):
        cs = slice(c * LANES, (c + 1) * LANES)
        uext_ref[c, :, POOL_PAD:POOL_PAD + tile, :] = u[:, cs].reshape(bt, tile, LANES)
        vext_ref[c, :, CONV_PAD:CONV_PAD + tile, :] = v[:, cs].reshape(bt, tile, LANES)

    def ffn_up(ci):
        blocks = FF_CHUNK // DOT_COLS
        up = _dot_cols(hb_ref[...], [w_up_ref[ci * blocks + n][...] for n in range(blocks)])
        ffn_events.append(_zero_bits_after(up[0:SUBLANES, 0:LANES]))
        return jnp.square(jnp.maximum(up, 0.0)).astype(bf16)

    ffn_events = []
    for ci in range(D_FF // FF_CHUNK):
        act_ref[:, ci * FF_CHUNK:(ci + 1) * FF_CHUNK] = ffn_up(ci)
    f = _dot_cols(act_ref[...], [w[...] for w in w_down_ref])
    y = _layer_norm(r_ref[...] + f, ln2_g_ref[...], ln2_b_ref[...])
    y_ref[...] = y.reshape(bt, tile, D_MODEL)

    pos1 = (pos0 + 1 + j * tile + jax.lax.broadcasted_iota(jnp.int32, (tile, LANES), 0)).astype(f32)
    pool_parts = []
    for gi, w in enumerate(POOL_WINDOWS):
        inv_count = 1.0 / jnp.minimum(pos1, float(w))
        wsum = uext_ref[gi, :, POOL_PAD:POOL_PAD + tile, :]
        for i in range(1, w):
            wsum = wsum + uext_ref[gi, :, POOL_PAD - i:POOL_PAD - i + tile, :]
        d = wsum * inv_count[None] - uext_ref[gi, :, POOL_PAD:POOL_PAD + tile, :]
        pool_parts.append(d.reshape(rows, POOL_GROUP).astype(bf16))
    pool_out = []
    for p in range(N_POOL_GROUPS // 2):
        d2 = jnp.concatenate(pool_parts[2 * p:2 * p + 2], axis=-1)
        pool_out.append(jnp.dot(d2, w_pool_ref[p], preferred_element_type=f32))
    pool_y = jnp.concatenate(pool_out, axis=-1) * pool_scale_ref[...]

    conv_parts = []
    for c in range(LANE_GROUPS):
        cs = slice(c * LANES, (c + 1) * LANES)
        s0 = CONV_PAD - CONV_HIST
        zero_bits = ffn_events[c][0:1, :]
        taps_bits = jax.lax.bitcast_convert_type(conv_w_ref[:, cs], jnp.uint32) + zero_bits
        taps_ref[0:CONV_KERNEL, cs] = jax.lax.bitcast_convert_type(taps_bits, f32)
        acc = vext_ref[c, :, s0:s0 + tile, :] * taps_ref[0:1, cs][None]
        for k in range(1, CONV_KERNEL):
            acc = acc + vext_ref[c, :, s0 + k:s0 + k + tile, :] * taps_ref[k:k + 1, cs][None]
        conv_parts.append(acc.reshape(rows, LANES))
    c = jnp.concatenate(conv_parts, axis=-1) + conv_b_ref[...]
    c = jax.nn.silu(_layer_norm(c, cln_g_ref[...], cln_b_ref[...]))

    mix = jnp.concatenate([pool_y.astype(bf16), c.astype(bf16)], axis=-1)
    proj = _dot_cols(mix, [w[...] for w in w_out_ref])
    h_ref[...] = alpha * x + proj + b_out_ref[...]

    for c in range(LANE_GROUPS):
        cs = slice(c * LANES, (c + 1) * LANES)
        u_tail = uext_ref[c, :, POOL_PAD + tile - POOL_HIST:POOL_PAD + tile, :]
        v_tail = vext_ref[c, :, CONV_PAD + tile - CONV_HIST:CONV_PAD + tile, :]
        new_pool_ref[:, :, cs] = u_tail
        new_conv_ref[:, :, cs] = v_tail
        uext_ref[c, :, POOL_PAD - POOL_HIST:POOL_PAD, :] = u_tail
        vext_ref[c, :, CONV_PAD - CONV_HIST:CONV_PAD, :] = v_tail


def _const_specs(param):
    array, by_col_blocks = param
    if not by_col_blocks:
        zeros = (0,) * array.ndim
        return [array], [pl.BlockSpec(array.shape, lambda s: zeros, pipeline_mode=pl.Buffered(1))]
    n_blocks = array.shape[1] // DOT_COLS
    specs = [pl.BlockSpec((array.shape[0], DOT_COLS), lambda s, n=n: (0, n), pipeline_mode=pl.Buffered(1))
             for n in range(n_blocks)]
    return [array] * n_blocks, specs


def _trunk_call(x, hist_pool, hist_conv, params, *, bt, tile, pos0, alpha, name):
    batch, seq, _ = x.shape
    assert batch % bt == 0 and seq % tile == 0 and tile % SUBLANES == 0
    n_tiles = seq // tile
    n_total = (batch // bt) * n_tiles
    grid = (n_total + 1,)
    mix_tile = lambda s: jnp.minimum(s, n_total - 1)
    ffn_tile = lambda s: jnp.maximum(s - 1, 0)
    kern = functools.partial(_trunk_kernel, bt=bt, tile=tile, n_tiles=n_tiles, pos0=pos0, alpha=alpha)
    param_arrays, param_specs = [], []
    for p in params:
        arrays, specs = _const_specs(p)
        param_arrays += arrays
        param_specs += specs
    in_specs = [
        pl.BlockSpec((bt, tile, D_MODEL), lambda s: (mix_tile(s) // n_tiles, mix_tile(s) % n_tiles, 0)),
        pl.BlockSpec((bt, POOL_HIST, POOL_WIDTH), lambda s: (mix_tile(s) // n_tiles, 0, 0)),
        pl.BlockSpec((bt, CONV_HIST, CONV_WIDTH), lambda s: (mix_tile(s) // n_tiles, 0, 0)),
    ] + param_specs
    out_specs = [
        pl.BlockSpec((bt, tile, D_MODEL), lambda s: (ffn_tile(s) // n_tiles, ffn_tile(s) % n_tiles, 0)),
        pl.BlockSpec((bt, POOL_HIST, POOL_WIDTH), lambda s: (mix_tile(s) // n_tiles, 0, 0)),
        pl.BlockSpec((bt, CONV_HIST, CONV_WIDTH), lambda s: (mix_tile(s) // n_tiles, 0, 0)),
    ]
    out_shape = [
        jax.ShapeDtypeStruct((batch, seq, D_MODEL), x.dtype),
        jax.ShapeDtypeStruct((batch, POOL_HIST, POOL_WIDTH), x.dtype),
        jax.ShapeDtypeStruct((batch, CONV_HIST, CONV_WIDTH), x.dtype),
    ]
    scratch = [
        pltpu.VMEM((LANE_GROUPS, bt, POOL_PAD + tile, LANES), jnp.float32),
        pltpu.VMEM((LANE_GROUPS, bt, CONV_PAD + tile, LANES), jnp.float32),
        pltpu.VMEM((bt * tile, D_MODEL), jnp.float32),
        pltpu.VMEM((bt * tile, D_MODEL), jnp.bfloat16),
        pltpu.VMEM((bt * tile, D_MODEL), jnp.float32),
        pltpu.VMEM((CONV_KERNEL + 1, CONV_WIDTH), jnp.float32),
        pltpu.VMEM((bt * tile, D_FF), jnp.bfloat16),
    ]
    return pl.pallas_call(
        kern, grid=grid, in_specs=in_specs, out_specs=out_specs, out_shape=out_shape,
        scratch_shapes=scratch, name=name,
        compiler_params=pltpu.CompilerParams(
            dimension_semantics=("arbitrary",),
            vmem_limit_bytes=VMEM_LIMIT_BYTES),
    )(x, hist_pool, hist_conv, *param_arrays)


def _layer_params(l, w_in, b_in, w_pool, pool_scale, conv_w, conv_b, conv_ln_g, conv_ln_b,
                  w_out, b_out, ln1_g, ln1_b, w_up, w_down, ln2_g, ln2_b):
    bf16 = jnp.bfloat16
    row = lambda p: p[l].reshape(1, -1)
    wp = w_pool[l].astype(bf16)
    zero = jnp.zeros((POOL_GROUP, POOL_GROUP), bf16)
    wp_pairs = jnp.stack([
        jnp.block([[wp[2 * p], zero], [zero, wp[2 * p + 1]]]) for p in range(N_POOL_GROUPS // 2)])
    whole = lambda a: (a, False)
    col_blocks = lambda w: (w.astype(bf16), True)
    return (whole(w_in[l].astype(bf16)), whole(row(b_in)), whole(wp_pairs), whole(row(pool_scale)),
            whole(conv_w[l].reshape(CONV_KERNEL, CONV_WIDTH)), whole(row(conv_b)),
            whole(row(conv_ln_g)), whole(row(conv_ln_b)),
            col_blocks(w_out[l]), whole(row(b_out)), whole(row(ln1_g)), whole(row(ln1_b)),
            col_blocks(w_up[l]), col_blocks(w_down[l]), whole(row(ln2_g)), whole(row(ln2_b)))


def kernel(x_prompt, x_sample, state_pool, state_conv, w_in, b_in, w_pool, pool_scale, conv_w, conv_b, conv_ln_g, conv_ln_b, w_out, b_out, ln1_g, ln1_b, w_up, w_down, ln2_g, ln2_b):
    depth = w_in.shape[0]
    alpha = (2.0 * depth) ** 0.25
    past_len = 1024
    yp, ys = x_prompt, x_sample
    pool_p, conv_p, pool_s, conv_s = [], [], [], []
    for l in range(depth):
        params = _layer_params(l, w_in, b_in, w_pool, pool_scale, conv_w, conv_b, conv_ln_g,
                               conv_ln_b, w_out, b_out, ln1_g, ln1_b, w_up, w_down, ln2_g, ln2_b)
        zp_pool = jnp.zeros((yp.shape[0], POOL_HIST, POOL_WIDTH), yp.dtype)
        zp_conv = jnp.zeros((yp.shape[0], CONV_HIST, CONV_WIDTH), yp.dtype)
        yp, hp, hc = _trunk_call(yp, zp_pool, zp_conv, params, bt=1, tile=512, pos0=0,
                                 alpha=alpha, name="trunk_prompt")
        pool_p.append(hp)
        conv_p.append(hc)
        ys, sp, sc = _trunk_call(ys, state_pool[l].astype(ys.dtype), state_conv[l].astype(ys.dtype),
                                 params, bt=ys.shape[0], tile=ys.shape[1], pos0=past_len,
                                 alpha=alpha, name="trunk_sample")
        pool_s.append(sp)
        conv_s.append(sc)
    stack = lambda parts: parts[0][None] if depth == 1 else jnp.stack(parts)
    return (yp, ys, stack(pool_p), stack(conv_p), stack(pool_s), stack(conv_s))
```

```python
import functools

import jax
import jax.numpy as jnp
from jax.experimental import pallas as pl
from jax.experimental.pallas import tpu as pltpu

D_MODEL = 1024
POOL_WINDOWS = (2, 4, 8, 16)
N_POOL_GROUPS = len(POOL_WINDOWS)
POOL_WIDTH = D_MODEL // 2
POOL_GROUP = POOL_WIDTH // N_POOL_GROUPS
CONV_WIDTH = D_MODEL - POOL_WIDTH
CONV_KERNEL = 31
POOL_HIST = max(POOL_WINDOWS) - 1
CONV_HIST = CONV_KERNEL - 1
D_FF = 4 * D_MODEL
D_IN = POOL_WIDTH + 2 * CONV_WIDTH
LN_EPS = 1e-5

SUBLANES = 8
LANES = 128
LANE_GROUPS = POOL_WIDTH // LANES
POOL_PAD = 16
CONV_PAD = 32
FF_CHUNK = 1024
DOT_COLS = 512
VMEM_LIMIT_BYTES = 56 * 1024 * 1024

assert POOL_GROUP == LANES and CONV_WIDTH == POOL_WIDTH


def _layer_norm(x, g, b):
    mu = jnp.mean(x, axis=-1, keepdims=True)
    xc = x - mu
    var = jnp.mean(xc * xc, axis=-1, keepdims=True)
    return xc * jax.lax.rsqrt(var + LN_EPS) * g + b


def _zero_bits_after(x):
    bits = jax.lax.bitcast_convert_type(x, jnp.uint32)
    return jax.lax.shift_right_logical(jax.lax.shift_right_logical(bits, jnp.uint32(16)), jnp.uint32(16))


def _dot_cols(lhs, w_blocks):
    return jnp.concatenate([jnp.dot(lhs, w, preferred_element_type=jnp.float32) for w in w_blocks], axis=-1)


N_OUT_BLOCKS = D_MODEL // DOT_COLS
N_UP_BLOCKS = D_FF // DOT_COLS


def _trunk_kernel(*refs, bt, tile, n_tiles, pos0, alpha):
    refs = list(refs)
    take = lambda n: [refs.pop(0) for _ in range(n)]
    (x_ref, hp_ref, hc_ref, w_in_ref, b_in_ref, w_pool_ref, pool_scale_ref,
     conv_w_ref, conv_b_ref, cln_g_ref, cln_b_ref) = take(11)
    w_out_ref = take(N_OUT_BLOCKS)
    b_out_ref, ln1_g_ref, ln1_b_ref = take(3)
    w_up_ref = take(N_UP_BLOCKS)
    w_down_ref = take(N_OUT_BLOCKS)
    ln2_g_ref, ln2_b_ref = take(2)
    y_ref, new_pool_ref, new_conv_ref = take(3)
    uext_ref, vext_ref, h_ref, hb_ref, r_ref, taps_ref, act_ref = refs
    s = pl.program_id(0)
    last_tile = pl.num_programs(0) - 2
    j = jax.lax.rem(jnp.minimum(s, last_tile), jnp.int32(n_tiles))
    rows = bt * tile
    f32 = jnp.float32
    bf16 = jnp.bfloat16

    @pl.when(s == 0)
    def _():
        h_ref[...] = jnp.zeros_like(h_ref)

    @pl.when(j == 0)
    def _():
        for c in range(LANE_GROUPS):
            cs = slice(c * LANES, (c + 1) * LANES)
            uext_ref[c, :, POOL_PAD - POOL_HIST:POOL_PAD, :] = hp_ref[:, :, cs]
            vext_ref[c, :, CONV_PAD - CONV_HIST:CONV_PAD, :] = hc_ref[:, :, cs]

    h_prev = _layer_norm(h_ref[...], ln1_g_ref[...], ln1_b_ref[...])
    hb_ref[...] = h_prev.astype(bf16)
    r_ref[...] = alpha * h_prev

    x = x_ref[...].reshape(rows, D_MODEL)
    z = jnp.dot(x.astype(bf16), w_in_ref[...], preferred_element_type=f32) + b_in_ref[...]
    u = z[:, :POOL_WIDTH]
    a = z[:, POOL_WIDTH:POOL_WIDTH + CONV_WIDTH]
    g = z[:, POOL_WIDTH + CONV_WIDTH:]
    v = a * jax.nn.sigmoid(g)
    for c in range(LANE_GROUPS):
        cs = slice(c * LANES, (c + 1) * LANES)
        uext_ref[c, :, POOL_PAD:POOL_PAD + tile, :] = u[:, cs].reshape(bt, tile, LANES)
        vext_ref[c, :, CONV_PAD:CONV_PAD + tile, :] = v[:, cs].reshape(bt, tile, LANES)

    def ffn_up(ci):
        blocks = FF_CHUNK // DOT_COLS
        up = _dot_cols(hb_ref[...], [w_up_ref[ci * blocks + n][...] for n in range(blocks)])
        ffn_events.append(_zero_bits_after(up[0:SUBLANES, 0:LANES]))
        return jnp.square(jnp.maximum(up, 0.0)).astype(bf16)

    ffn_events = []
    for ci in range(D_FF // FF_CHUNK):
        act_ref[:, ci * FF_CHUNK:(ci + 1) * FF_CHUNK] = ffn_up(ci)
    f = _dot_cols(act_ref[...], [w[...] for w in w_down_ref])
    y = _layer_norm(r_ref[...] + f, ln2_g_ref[...], ln2_b_ref[...])
    y_ref[...] = y.reshape(bt, tile, D_MODEL)

    pos1 = (pos0 + 1 + j * tile + jax.lax.broadcasted_iota(jnp.int32, (tile, LANES), 0)).astype(f32)
    pool_parts = []
    for gi, w in enumerate(POOL_WINDOWS):
        inv_count = 1.0 / jnp.minimum(pos1, float(w))
        wsum = uext_ref[gi, :, POOL_PAD:POOL_PAD + tile, :]
        for i in range(1, w):
            wsum = wsum + uext_ref[gi, :, POOL_PAD - i:POOL_PAD - i + tile, :]
        d = wsum * inv_count[None] - uext_ref[gi, :, POOL_PAD:POOL_PAD + tile, :]
        pool_parts.append(d.reshape(rows, POOL_GROUP).astype(bf16))
    pool_out = []
    for p in range(N_POOL_GROUPS // 2):
        d2 = jnp.concatenate(pool_parts[2 * p:2 * p + 2], axis=-1)
        pool_out.append(jnp.dot(d2, w_pool_ref[p], preferred_element_type=f32))
    pool_y = jnp.concatenate(pool_out, axis=-1) * pool_scale_ref[...]

    conv_parts = []
    for c in range(LANE_GROUPS):
        cs = slice(c * LANES, (c + 1) * LANES)
        s0 = CONV_PAD - CONV_HIST
        zero_bits = ffn_events[c][0:1, :]
        taps_bits = jax.lax.bitcast_convert_type(conv_w_ref[:, cs], jnp.uint32) + zero_bits
        taps_ref[0:CONV_KERNEL, cs] = jax.lax.bitcast_convert_type(taps_bits, f32)
        acc = vext_ref[c, :, s0:s0 + tile, :] * taps_ref[0:1, cs][None]
        for k in range(1, CONV_KERNEL):
            acc = acc + vext_ref[c, :, s0 + k:s0 + k + tile, :] * taps_ref[k:k + 1, cs][None]
        conv_parts.append(acc.reshape(rows, LANES))
    c = jnp.concatenate(conv_parts, axis=-1) + conv_b_ref[...]
    c = jax.nn.silu(_layer_norm(c, cln_g_ref[...], cln_b_ref[...]))

    mix = jnp.concatenate([pool_y.astype(bf16), c.astype(bf16)], axis=-1)
    proj = _dot_cols(mix, [w[...] for w in w_out_ref])
    h_ref[...] = alpha * x + proj + b_out_ref[...]

    for c in range(LANE_GROUPS):
        cs = slice(c * LANES, (c + 1) * LANES)
        u_tail = uext_ref[c, :, POOL_PAD + tile - POOL_HIST:POOL_PAD + tile, :]
        v_tail = vext_ref[c, :, CONV_PAD + tile - CONV_HIST:CONV_PAD + tile, :]
        new_pool_ref[:, :, cs] = u_tail
        new_conv_ref[:, :, cs] = v_tail
        uext_ref[c, :, POOL_PAD - POOL_HIST:POOL_PAD, :] = u_tail
        vext_ref[c, :, CONV_PAD - CONV_HIST:CONV_PAD, :] = v_tail


def _const_specs(param):
    array, by_col_blocks = param
    if not by_col_blocks:
        return [array], [pl.BlockSpec(memory_space=pltpu.VMEM)]
    n_blocks = array.shape[1] // DOT_COLS
    specs = [pl.BlockSpec((array.shape[0], DOT_COLS), lambda s, n=n: (0, n), pipeline_mode=pl.Buffered(1))
             for n in range(n_blocks)]
    return [array] * n_blocks, specs


def _trunk_call(x, hist_pool, hist_conv, params, *, bt, tile, pos0, alpha, name):
    batch, seq, _ = x.shape
    assert batch % bt == 0 and seq % tile == 0 and tile % SUBLANES == 0
    n_tiles = seq // tile
    n_total = (batch // bt) * n_tiles
    grid = (n_total + 1,)
    mix_tile = lambda s: jnp.minimum(s, n_total - 1)
    ffn_tile = lambda s: jnp.maximum(s - 1, 0)
    stream_of = lambda t: jax.lax.div(t, jnp.int32(n_tiles))
    pos_of = lambda t: jax.lax.rem(t, jnp.int32(n_tiles))
    kern = functools.partial(_trunk_kernel, bt=bt, tile=tile, n_tiles=n_tiles, pos0=pos0, alpha=alpha)
    param_arrays, param_specs = [], []
    for p in params:
        arrays, specs = _const_specs(p)
        param_arrays += arrays
        param_specs += specs
    in_specs = [
        pl.BlockSpec((bt, tile, D_MODEL), lambda s: (stream_of(mix_tile(s)), pos_of(mix_tile(s)), 0)),
        pl.BlockSpec((bt, POOL_HIST, POOL_WIDTH), lambda s: (stream_of(mix_tile(s)), 0, 0)),
        pl.BlockSpec((bt, CONV_HIST, CONV_WIDTH), lambda s: (stream_of(mix_tile(s)), 0, 0)),
    ] + param_specs
    out_specs = [
        pl.BlockSpec((bt, tile, D_MODEL), lambda s: (stream_of(ffn_tile(s)), pos_of(ffn_tile(s)), 0)),
        pl.BlockSpec((bt, POOL_HIST, POOL_WIDTH), lambda s: (stream_of(mix_tile(s)), 0, 0)),
        pl.BlockSpec((bt, CONV_HIST, CONV_WIDTH), lambda s: (stream_of(mix_tile(s)), 0, 0)),
    ]
    out_shape = [
        jax.ShapeDtypeStruct((batch, seq, D_MODEL), x.dtype),
        jax.ShapeDtypeStruct((batch, POOL_HIST, POOL_WIDTH), x.dtype),
        jax.ShapeDtypeStruct((batch, CONV_HIST, CONV_WIDTH), x.dtype),
    ]
    scratch = [
        pltpu.VMEM((LANE_GROUPS, bt, POOL_PAD + tile, LANES), jnp.float32),
        pltpu.VMEM((LANE_GROUPS, bt, CONV_PAD + tile, LANES), jnp.float32),
        pltpu.VMEM((bt * tile, D_MODEL), jnp.float32),
        pltpu.VMEM((bt * tile, D_MODEL), jnp.bfloat16),
        pltpu.VMEM((bt * tile, D_MODEL), jnp.float32),
        pltpu.VMEM((CONV_KERNEL + 1, CONV_WIDTH), jnp.float32),
        pltpu.VMEM((bt * tile, D_FF), jnp.bfloat16),
    ]
    return pl.pallas_call(
        kern, grid=grid, in_specs=in_specs, out_specs=out_specs, out_shape=out_shape,
        scratch_shapes=scratch, name=name,
        compiler_params=pltpu.CompilerParams(
            dimension_semantics=("arbitrary",),
            vmem_limit_bytes=VMEM_LIMIT_BYTES),
    )(x, hist_pool, hist_conv, *param_arrays)


def _layer_params(l, w_in, b_in, w_pool, pool_scale, conv_w, conv_b, conv_ln_g, conv_ln_b,
                  w_out, b_out, ln1_g, ln1_b, w_up, w_down, ln2_g, ln2_b):
    bf16 = jnp.bfloat16
    row = lambda p: p[l].reshape(1, -1)
    wp = w_pool[l].astype(bf16)
    zero = jnp.zeros((POOL_GROUP, POOL_GROUP), bf16)
    wp_pairs = jnp.stack([
        jnp.block([[wp[2 * p], zero], [zero, wp[2 * p + 1]]]) for p in range(N_POOL_GROUPS // 2)])
    whole = lambda a: (a, False)
    col_blocks = lambda w: (w.astype(bf16), True)
    return (whole(w_in[l].astype(bf16)), whole(row(b_in)), whole(wp_pairs), whole(row(pool_scale)),
            whole(conv_w[l].reshape(CONV_KERNEL, CONV_WIDTH)), whole(row(conv_b)),
            whole(row(conv_ln_g)), whole(row(conv_ln_b)),
            col_blocks(w_out[l]), whole(row(b_out)), whole(row(ln1_g)), whole(row(ln1_b)),
            col_blocks(w_up[l]), col_blocks(w_down[l]), whole(row(ln2_g)), whole(row(ln2_b)))


def kernel(x_prompt, x_sample, state_pool, state_conv, w_in, b_in, w_pool, pool_scale, conv_w, conv_b, conv_ln_g, conv_ln_b, w_out, b_out, ln1_g, ln1_b, w_up, w_down, ln2_g, ln2_b):
    depth = w_in.shape[0]
    alpha = (2.0 * depth) ** 0.25
    past_len = 1024
    yp, ys = x_prompt, x_sample
    pool_p, conv_p, pool_s, conv_s = [], [], [], []
    for l in range(depth):
        params = _layer_params(l, w_in, b_in, w_pool, pool_scale, conv_w, conv_b, conv_ln_g,
                               conv_ln_b, w_out, b_out, ln1_g, ln1_b, w_up, w_down, ln2_g, ln2_b)
        zp_pool = jnp.zeros((yp.shape[0], POOL_HIST, POOL_WIDTH), yp.dtype)
        zp_conv = jnp.zeros((yp.shape[0], CONV_HIST, CONV_WIDTH), yp.dtype)
        yp, hp, hc = _trunk_call(yp, zp_pool, zp_conv, params, bt=1, tile=512, pos0=0,
                                 alpha=alpha, name="trunk_prompt")
        pool_p.append(hp)
        conv_p.append(hc)
        ys, sp, sc = _trunk_call(ys, state_pool[l].astype(ys.dtype), state_conv[l].astype(ys.dtype),
                                 params, bt=ys.shape[0], tile=ys.shape[1], pos0=past_len,
                                 alpha=alpha, name="trunk_sample")
        pool_s.append(sp)
        conv_s.append(sc)
    stack = lambda parts: parts[0][None] if depth == 1 else jnp.stack(parts)
    return (yp, ys, stack(pool_p), stack(conv_p), stack(pool_s), stack(conv_s))
```

```python
import functools

import jax
import jax.numpy as jnp
from jax.experimental import pallas as pl
from jax.experimental.pallas import tpu as pltpu

D_MODEL = 1024
POOL_WINDOWS = (2, 4, 8, 16)
N_POOL_GROUPS = len(POOL_WINDOWS)
POOL_WIDTH = D_MODEL // 2
POOL_GROUP = POOL_WIDTH // N_POOL_GROUPS
CONV_WIDTH = D_MODEL - POOL_WIDTH
CONV_KERNEL = 31
POOL_HIST = max(POOL_WINDOWS) - 1
CONV_HIST = CONV_KERNEL - 1
D_FF = 4 * D_MODEL
D_IN = POOL_WIDTH + 2 * CONV_WIDTH
LN_EPS = 1e-5

SUBLANES = 8
LANES = 128
LANE_GROUPS = POOL_WIDTH // LANES
POOL_PAD = 16
CONV_PAD = 32
FF_CHUNK = 1024
DOT_COLS = 512
OUT_LAG = 2
VMEM_LIMIT_BYTES = 56 * 1024 * 1024

N_OUT_BLOCKS = D_MODEL // DOT_COLS
N_UP_BLOCKS = D_FF // DOT_COLS

assert POOL_GROUP == LANES and CONV_WIDTH == POOL_WIDTH


def _layer_norm(x, g, b):
    mu = jnp.mean(x, axis=-1, keepdims=True)
    xc = x - mu
    var = jnp.mean(xc * xc, axis=-1, keepdims=True)
    return xc * jax.lax.rsqrt(var + LN_EPS) * g + b


def _zero_bits_after(x):
    bits = jax.lax.bitcast_convert_type(x, jnp.uint32)
    return jax.lax.shift_right_logical(jax.lax.shift_right_logical(bits, jnp.uint32(16)), jnp.uint32(16))


def _dot_cols(lhs, w_blocks):
    return jnp.concatenate([jnp.dot(lhs, w, preferred_element_type=jnp.float32) for w in w_blocks], axis=-1)


def _trunk_kernel(*refs, bt, tile, sub, n_tiles, n_total, pos0, alpha):
    refs = list(refs)
    take = lambda n: [refs.pop(0) for _ in range(n)]
    (x_ref, hp_ref, hc_ref, w_in_ref, b_in_ref, w_pool_ref, pool_scale_ref,
     conv_w_ref, conv_b_ref, cln_g_ref, cln_b_ref) = take(11)
    w_out_ref = take(N_OUT_BLOCKS)
    b_out_ref, ln1_g_ref, ln1_b_ref = take(3)
    w_up_ref = take(N_UP_BLOCKS)
    w_down_ref = take(N_OUT_BLOCKS)
    ln2_g_ref, ln2_b_ref = take(2)
    y_ref, new_pool_ref, new_conv_ref = take(3)
    uext_ref, vext_ref, h_ref, hb_ref, r_ref, taps_ref, act_ref, ystage_ref = refs
    rows = bt * tile
    f32 = jnp.float32
    bf16 = jnp.bfloat16

    def tile_step(i):
        t = pl.program_id(0) * sub + i
        j = jax.lax.rem(jnp.minimum(t, n_total - 1), jnp.int32(n_tiles))
        row0 = i * tile if isinstance(i, int) else pl.multiple_of(i * tile, tile)

        @pl.when(t == 0)
        def _():
            h_ref[...] = jnp.zeros_like(h_ref)
            ystage_ref[...] = jnp.zeros_like(ystage_ref)

        @pl.when(j == 0)
        def _():
            for c in range(LANE_GROUPS):
                cs = slice(c * LANES, (c + 1) * LANES)
                uext_ref[c, :, POOL_PAD - POOL_HIST:POOL_PAD, :] = hp_ref[:, :, cs]
                vext_ref[c, :, CONV_PAD - CONV_HIST:CONV_PAD, :] = hc_ref[:, :, cs]

        y_ref[:, pl.ds(row0, tile), :] = ystage_ref[...].reshape(bt, tile, D_MODEL)

        h_prev = _layer_norm(h_ref[...], ln1_g_ref[...], ln1_b_ref[...])
        hb_ref[...] = h_prev.astype(bf16)
        r_ref[...] = alpha * h_prev

        x = x_ref[:, pl.ds(row0, tile), :].reshape(rows, D_MODEL)
        z = jnp.dot(x.astype(bf16), w_in_ref[...], preferred_element_type=f32) + b_in_ref[...]
        u = z[:, :POOL_WIDTH]
        a = z[:, POOL_WIDTH:POOL_WIDTH + CONV_WIDTH]
        g = z[:, POOL_WIDTH + CONV_WIDTH:]
        v = a * jax.nn.sigmoid(g)
        for c in range(LANE_GROUPS):
            cs = slice(c * LANES, (c + 1) * LANES)
            uext_ref[c, :, POOL_PAD:POOL_PAD + tile, :] = u[:, cs].reshape(bt, tile, LANES)
            vext_ref[c, :, CONV_PAD:CONV_PAD + tile, :] = v[:, cs].reshape(bt, tile, LANES)

        def ffn_up(ci):
            blocks = FF_CHUNK // DOT_COLS
            up = _dot_cols(hb_ref[...], [w_up_ref[ci * blocks + n][...] for n in range(blocks)])
            ffn_events.append(_zero_bits_after(up[0:SUBLANES, 0:LANES]))
            return jnp.square(jnp.maximum(up, 0.0)).astype(bf16)

        ffn_events = []
        for ci in range(D_FF // FF_CHUNK):
            act_ref[:, ci * FF_CHUNK:(ci + 1) * FF_CHUNK] = ffn_up(ci)
        f = _dot_cols(act_ref[...], [w[...] for w in w_down_ref])
        ystage_ref[...] = _layer_norm(r_ref[...] + f, ln2_g_ref[...], ln2_b_ref[...])

        pos1 = (pos0 + 1 + j * tile + jax.lax.broadcasted_iota(jnp.int32, (tile, LANES), 0)).astype(f32)
        pool_parts = []
        for gi, w in enumerate(POOL_WINDOWS):
            inv_count = 1.0 / jnp.minimum(pos1, float(w))
            wsum = uext_ref[gi, :, POOL_PAD:POOL_PAD + tile, :]
            for back in range(1, w):
                wsum = wsum + uext_ref[gi, :, POOL_PAD - back:POOL_PAD - back + tile, :]
            d = wsum * inv_count[None] - uext_ref[gi, :, POOL_PAD:POOL_PAD + tile, :]
            pool_parts.append(d.reshape(rows, POOL_GROUP).astype(bf16))
        pool_out = []
        for p in range(N_POOL_GROUPS // 2):
            d2 = jnp.concatenate(pool_parts[2 * p:2 * p + 2], axis=-1)
            pool_out.append(jnp.dot(d2, w_pool_ref[p], preferred_element_type=f32))
        pool_y = jnp.concatenate(pool_out, axis=-1) * pool_scale_ref[...]

        conv_parts = []
        for c in range(LANE_GROUPS):
            cs = slice(c * LANES, (c + 1) * LANES)
            s0 = CONV_PAD - CONV_HIST
            zero_bits = ffn_events[c][0:1, :]
            taps_bits = jax.lax.bitcast_convert_type(conv_w_ref[:, cs], jnp.uint32) + zero_bits
            taps_ref[0:CONV_KERNEL, cs] = jax.lax.bitcast_convert_type(taps_bits, f32)
            acc = vext_ref[c, :, s0:s0 + tile, :] * taps_ref[0:1, cs][None]
            for k in range(1, CONV_KERNEL):
                acc = acc + vext_ref[c, :, s0 + k:s0 + k + tile, :] * taps_ref[k:k + 1, cs][None]
            conv_parts.append(acc.reshape(rows, LANES))
        cv = jnp.concatenate(conv_parts, axis=-1) + conv_b_ref[...]
        cv = jax.nn.silu(_layer_norm(cv, cln_g_ref[...], cln_b_ref[...]))

        mix = jnp.concatenate([pool_y.astype(bf16), cv.astype(bf16)], axis=-1)
        proj = _dot_cols(mix, [w[...] for w in w_out_ref])
        h_ref[...] = alpha * x + proj + b_out_ref[...]

        for c in range(LANE_GROUPS):
            cs = slice(c * LANES, (c + 1) * LANES)
            u_tail = uext_ref[c, :, POOL_PAD + tile - POOL_HIST:POOL_PAD + tile, :]
            v_tail = vext_ref[c, :, CONV_PAD + tile - CONV_HIST:CONV_PAD + tile, :]
            new_pool_ref[:, :, cs] = u_tail
            new_conv_ref[:, :, cs] = v_tail
            uext_ref[c, :, POOL_PAD - POOL_HIST:POOL_PAD, :] = u_tail
            vext_ref[c, :, CONV_PAD - CONV_HIST:CONV_PAD, :] = v_tail

    if sub == 1:
        tile_step(0)
    else:
        def loop_body(i, carry):
            tile_step(i)
            return carry
        jax.lax.fori_loop(0, sub, loop_body, 0)


def _const_specs(param):
    array, by_col_blocks = param
    if not by_col_blocks:
        return [array], [pl.BlockSpec(memory_space=pltpu.VMEM)]
    n_blocks = array.shape[1] // DOT_COLS
    specs = [pl.BlockSpec((array.shape[0], DOT_COLS), lambda g, n=n: (0, n), pipeline_mode=pl.Buffered(1))
             for n in range(n_blocks)]
    return [array] * n_blocks, specs


def _trunk_call(x, hist_pool, hist_conv, params, *, bt, tile, sub, pos0, alpha, name):
    batch, seq, _ = x.shape
    assert batch % bt == 0 and seq % (sub * tile) == 0 and tile % SUBLANES == 0
    n_tiles = seq // tile
    n_total = (batch // bt) * n_tiles
    assert (n_total + OUT_LAG) % sub == 0 and OUT_LAG % sub == 0
    grid = ((n_total + OUT_LAG) // sub,)
    win_per_stream = n_tiles // sub
    n_windows = n_total // sub
    in_window = lambda g: jnp.minimum(g, n_windows - 1)
    out_window = lambda g: jnp.maximum(g - OUT_LAG // sub, 0)
    stream_of = lambda w: jax.lax.div(w, jnp.int32(win_per_stream))
    pos_of = lambda w: jax.lax.rem(w, jnp.int32(win_per_stream))
    kern = functools.partial(_trunk_kernel, bt=bt, tile=tile, sub=sub, n_tiles=n_tiles, n_total=n_total,
                             pos0=pos0, alpha=alpha)
    param_arrays, param_specs = [], []
    for p in params:
        arrays, specs = _const_specs(p)
        param_arrays += arrays
        param_specs += specs
    in_specs = [
        pl.BlockSpec((bt, sub * tile, D_MODEL), lambda g: (stream_of(in_window(g)), pos_of(in_window(g)), 0)),
        pl.BlockSpec((bt, POOL_HIST, POOL_WIDTH), lambda g: (stream_of(in_window(g)), 0, 0)),
        pl.BlockSpec((bt, CONV_HIST, CONV_WIDTH), lambda g: (stream_of(in_window(g)), 0, 0)),
    ] + param_specs
    out_specs = [
        pl.BlockSpec((bt, sub * tile, D_MODEL), lambda g: (stream_of(out_window(g)), pos_of(out_window(g)), 0)),
        pl.BlockSpec((bt, POOL_HIST, POOL_WIDTH), lambda g: (stream_of(in_window(g)), 0, 0)),
        pl.BlockSpec((bt, CONV_HIST, CONV_WIDTH), lambda g: (stream_of(in_window(g)), 0, 0)),
    ]
    out_shape = [
        jax.ShapeDtypeStruct((batch, seq, D_MODEL), x.dtype),
        jax.ShapeDtypeStruct((batch, POOL_HIST, POOL_WIDTH), x.dtype),
        jax.ShapeDtypeStruct((batch, CONV_HIST, CONV_WIDTH), x.dtype),
    ]
    scratch = [
        pltpu.VMEM((LANE_GROUPS, bt, POOL_PAD + tile, LANES), jnp.float32),
        pltpu.VMEM((LANE_GROUPS, bt, CONV_PAD + tile, LANES), jnp.float32),
        pltpu.VMEM((bt * tile, D_MODEL), jnp.float32),
        pltpu.VMEM((bt * tile, D_MODEL), jnp.bfloat16),
        pltpu.VMEM((bt * tile, D_MODEL), jnp.float32),
        pltpu.VMEM((CONV_KERNEL + 1, CONV_WIDTH), jnp.float32),
        pltpu.VMEM((bt * tile, D_FF), jnp.bfloat16),
        pltpu.VMEM((bt * tile, D_MODEL), jnp.float32),
    ]
    return pl.pallas_call(
        kern, grid=grid, in_specs=in_specs, out_specs=out_specs, out_shape=out_shape,
        scratch_shapes=scratch, name=name,
        compiler_params=pltpu.CompilerParams(
            dimension_semantics=("arbitrary",),
            vmem_limit_bytes=VMEM_LIMIT_BYTES),
    )(x, hist_pool, hist_conv, *param_arrays)


def _layer_params(l, w_in, b_in, w_pool, pool_scale, conv_w, conv_b, conv_ln_g, conv_ln_b,
                  w_out, b_out, ln1_g, ln1_b, w_up, w_down, ln2_g, ln2_b):
    bf16 = jnp.bfloat16
    row = lambda p: p[l].reshape(1, -1)
    wp = w_pool[l].astype(bf16)
    zero = jnp.zeros((POOL_GROUP, POOL_GROUP), bf16)
    wp_pairs = jnp.stack([
        jnp.block([[wp[2 * p], zero], [zero, wp[2 * p + 1]]]) for p in range(N_POOL_GROUPS // 2)])
    whole = lambda a: (a, False)
    col_blocks = lambda w: (w.astype(bf16), True)
    return (whole(w_in[l].astype(bf16)), whole(row(b_in)), whole(wp_pairs), whole(row(pool_scale)),
            whole(conv_w[l].reshape(CONV_KERNEL, CONV_WIDTH)), whole(row(conv_b)),
            whole(row(conv_ln_g)), whole(row(conv_ln_b)),
            col_blocks(w_out[l]), whole(row(b_out)), whole(row(ln1_g)), whole(row(ln1_b)),
            col_blocks(w_up[l]), col_blocks(w_down[l]), whole(row(ln2_g)), whole(row(ln2_b)))


def kernel(x_prompt, x_sample, state_pool, state_conv, w_in, b_in, w_pool, pool_scale, conv_w, conv_b, conv_ln_g, conv_ln_b, w_out, b_out, ln1_g, ln1_b, w_up, w_down, ln2_g, ln2_b):
    depth = w_in.shape[0]
    alpha = (2.0 * depth) ** 0.25
    past_len = 1024
    yp, ys = x_prompt, x_sample
    pool_p, conv_p, pool_s, conv_s = [], [], [], []
    for l in range(depth):
        params = _layer_params(l, w_in, b_in, w_pool, pool_scale, conv_w, conv_b, conv_ln_g,
                               conv_ln_b, w_out, b_out, ln1_g, ln1_b, w_up, w_down, ln2_g, ln2_b)
        zp_pool = jnp.zeros((yp.shape[0], POOL_HIST, POOL_WIDTH), yp.dtype)
        zp_conv = jnp.zeros((yp.shape[0], CONV_HIST, CONV_WIDTH), yp.dtype)
        yp, hp, hc = _trunk_call(yp, zp_pool, zp_conv, params, bt=1, tile=512, sub=2, pos0=0,
                                 alpha=alpha, name="trunk_prompt")
        pool_p.append(hp)
        conv_p.append(hc)
        ys, sp, sc = _trunk_call(ys, state_pool[l].astype(ys.dtype), state_conv[l].astype(ys.dtype),
                                 params, bt=ys.shape[0], tile=ys.shape[1], sub=1, pos0=past_len,
                                 alpha=alpha, name="trunk_sample")
        pool_s.append(sp)
        conv_s.append(sc)
    stack = lambda parts: parts[0][None] if depth == 1 else jnp.stack(parts)
    return (yp, ys, stack(pool_p), stack(conv_p), stack(pool_s), stack(conv_s))
```
